```python
import jax, jax.numpy as jnp
from jax import lax
import numpy as np

D_MODEL = 2048
BATCH = 2
SEQ = 16384
DEPTH = 1

HEAD_DIM = 128
MIX_WIDTH = D_MODEL
GDN_WIDTH = MIX_WIDTH // 2
MOBA_WIDTH = MIX_WIDTH - GDN_WIDTH
GDN_HEADS = GDN_WIDTH // HEAD_DIM
MOBA_HEADS = MOBA_WIDTH // HEAD_DIM
CONV_WIDTH = 4
GDN_CHUNK = 64
MOBA_BLOCK = 256
MOBA_TOPK = 3
MOBA_QCHUNK = 16
MEM_LEN = 256
MEM_HEADS = 4
MEM_WIDTH = MEM_HEADS * HEAD_DIM
D_FF = 4 * D_MODEL
NORM_EPS = 1e-6
IN_SPLITS = (GDN_WIDTH, GDN_WIDTH, GDN_WIDTH, GDN_WIDTH, GDN_HEADS, GDN_HEADS,
             MOBA_WIDTH, MOBA_WIDTH, MOBA_WIDTH)
IN_WIDTH = sum(IN_SPLITS)

kernel_name = 'hybrid_gdn_moba_block'

F32 = jnp.float32


def rms_norm(x, w):
    xf = x.astype(F32)
    y = xf * lax.rsqrt(jnp.mean(xf * xf, axis=-1, keepdims=True) + NORM_EPS)
    return (y * w.astype(F32)).astype(x.dtype)


def l2_norm(x):
    return x * lax.rsqrt(jnp.sum(x * x, axis=-1, keepdims=True) + NORM_EPS)


def alibi_slopes(n):
    return jnp.exp2(-8.0 * jnp.arange(1, n + 1, dtype=F32) / n)


def causal_depthwise_conv(x, w):
    c = x.shape[-1]
    return lax.conv_general_dilated(
        x, w.reshape(CONV_WIDTH, 1, c).astype(x.dtype), window_strides=(1,),
        padding=[(CONV_WIDTH - 1, 0)], dimension_numbers=('NWC', 'WIO', 'NWC'),
        feature_group_count=c)


def gated_delta_rule_chunked(q, k, v, g, beta):
    b, h, t, dk = q.shape
    dv = v.shape[-1]
    n = t // GDN_CHUNK
    q, k, v = (a.reshape(b, h, n, GDN_CHUNK, -1) for a in (q, k, v))
    g = jnp.cumsum(g.reshape(b, h, n, GDN_CHUNK), axis=-1)
    beta = beta.reshape(b, h, n, GDN_CHUNK)
    idx = jnp.arange(GDN_CHUNK)
    causal = idx[:, None] >= idx[None, :]
    strict = idx[:, None] > idx[None, :]
    decay = jnp.exp(jnp.where(causal, g[..., :, None] - g[..., None, :], -jnp.inf))
    k_beta = k * beta[..., None]
    m = jnp.where(strict, jnp.einsum('bhncd,bhnsd->bhncs', k_beta, k) * decay, 0.0)
    eye = jnp.eye(GDN_CHUNK, dtype=F32)
    t_mat = lax.linalg.triangular_solve(eye + m, jnp.broadcast_to(eye, m.shape),
                                        left_side=True, lower=True, unit_diagonal=True)
    u = jnp.einsum('bhncs,bhnsd->bhncd', t_mat, v * beta[..., None])
    w = jnp.einsum('bhncs,bhnsd->bhncd', t_mat, k_beta * jnp.exp(g)[..., None])
    a_qk = jnp.einsum('bhncd,bhnsd->bhncs', q, k) * decay
    q_g = q * jnp.exp(g)[..., None]
    k_tail = k * jnp.exp(g[..., -1:] - g)[..., None]
    g_last = jnp.exp(g[..., -1])

    def step(s, inp):
        u_n, w_n, qg_n, a_n, kt_n, gl_n = inp
        v_new = u_n - jnp.einsum('bhck,bhkv->bhcv', w_n, s)
        o = jnp.einsum('bhck,bhkv->bhcv', qg_n, s) + jnp.einsum('bhcs,bhsv->bhcv', a_n, v_new)
        s = s * gl_n[..., None, None] + jnp.einsum('bhck,bhcv->bhkv', kt_n, v_new)
        return s, o

    xs = tuple(jnp.moveaxis(a, 2, 0) for a in (u, w, q_g, a_qk, k_tail, g_last))
    s0 = jnp.zeros((b, h, dk, dv), F32)
    _, o = lax.scan(step, s0, xs)
    return jnp.moveaxis(o, 0, 2).reshape(b, h, t, dv)


def gdn_branch(q, k, v, z, a, bt, conv_w, a_log, dt_bias, norm_w):
    b, t, _ = q.shape
    qkv = jax.nn.silu(causal_depthwise_conv(jnp.concatenate([q, k, v], axis=-1), conv_w))
    q, k, v = jnp.split(qkv, 3, axis=-1)
    heads = lambda y: y.reshape(b, t, GDN_HEADS, HEAD_DIM).transpose(0, 2, 1, 3).astype(F32)
    q = l2_norm(heads(q)) * HEAD_DIM ** -0.5
    k = l2_norm(heads(k))
    v = heads(v)
    beta = jax.nn.sigmoid(bt.astype(F32)).transpose(0, 2, 1)
    g = -(jnp.exp(a_log.astype(F32)) * jax.nn.softplus(a.astype(F32) + dt_bias.astype(F32)))
    o = gated_delta_rule_chunked(q, k, v, g.transpose(0, 2, 1), beta).transpose(0, 2, 1, 3)
    gate = jax.nn.silu(z.reshape(b, t, GDN_HEADS, HEAD_DIM).astype(F32))
    o = rms_norm(o, norm_w) * gate
    return o.reshape(b, t, GDN_WIDTH).astype(z.dtype)


def moba_attention(q, k, v, slopes):
    b, h, t, d = q.shape
    tp = -(-t // MOBA_BLOCK) * MOBA_BLOCK
    pad = ((0, 0), (0, 0), (0, tp - t), (0, 0))
    q, k, v = jnp.pad(q, pad), jnp.pad(k, pad), jnp.pad(v, pad)
    nb = tp // MOBA_BLOCK
    kb = k.reshape(b, h, nb, MOBA_BLOCK, d)
    vb = v.reshape(b, h, nb, MOBA_BLOCK, d)
    kmean = jnp.mean(kb.astype(F32), axis=3)
    topk = min(MOBA_TOPK, nb)
    bi = jnp.arange(b)[:, None, None, None]
    hi = jnp.arange(h)[None, :, None, None]
    blk_ids = jnp.arange(nb)
    in_blk = jnp.arange(MOBA_BLOCK)

    def one_chunk(c):
        t0 = c * MOBA_QCHUNK
        blk = t0 // MOBA_BLOCK
        qc = lax.dynamic_slice_in_dim(q, t0, MOBA_QCHUNK, axis=2)
        tpos = t0 + jnp.arange(MOBA_QCHUNK)
        gate = jnp.einsum('bhqd,bhnd->bhqn', qc.astype(F32), kmean)
        gate = jnp.where(blk_ids < blk, gate, -jnp.inf)
        _, sel = lax.top_k(gate, topk)
        valid = jnp.arange(topk) < blk
        ks = kb[bi, hi, sel]
        vs = vb[bi, hi, sel]
        s_sel = jnp.einsum('bhqd,bhqjsd->bhqjs', qc, ks).astype(F32)
        dist_sel = (tpos[None, None, :, None, None]
                    - (sel[..., None] * MOBA_BLOCK + in_blk)).astype(F32)
        s_sel = jnp.where(valid[:, None], s_sel - slopes[None, :, None, None, None] * dist_sel,
                          -jnp.inf)
        ko = lax.dynamic_slice_in_dim(k, blk * MOBA_BLOCK, MOBA_BLOCK, axis=2)
        vo = lax.dynamic_slice_in_dim(v, blk * MOBA_BLOCK, MOBA_BLOCK, axis=2)
        dist_own = tpos[:, None] - (blk * MOBA_BLOCK + in_blk)[None, :]
        s_own = jnp.einsum('bhqd,bhsd->bhqs', qc, ko).astype(F32)
        s_own = jnp.where(dist_own >= 0,
                          s_own - slopes[None, :, None, None] * dist_own.astype(F32), -jnp.inf)
        n_sel = topk * MOBA_BLOCK
        p = jax.nn.softmax(jnp.concatenate(
            [s_sel.reshape(b, h, MOBA_QCHUNK, n_sel), s_own], axis=-1), axis=-1).astype(v.dtype)
        p_sel = p[..., :n_sel].reshape(b, h, MOBA_QCHUNK, topk, MOBA_BLOCK)
        return (jnp.einsum('bhqjs,bhqjsd->bhqd', p_sel, vs)
                + jnp.einsum('bhqs,bhsd->bhqd', p[..., n_sel:], vo))

    out = lax.map(one_chunk, jnp.arange(tp // MOBA_QCHUNK))
    return jnp.moveaxis(out, 0, 2).reshape(b, h, tp, d)[:, :, :t]


def moba_branch(q, k, v):
    b, t, _ = q.shape
    heads = lambda y: y.reshape(b, t, MOBA_HEADS, HEAD_DIM).transpose(0, 2, 1, 3)
    o = moba_attention(heads(q) * HEAD_DIM ** -0.5, heads(k), heads(v), alibi_slopes(MOBA_HEADS))
    return o.transpose(0, 2, 1, 3).reshape(b, t, MOBA_WIDTH)


def memory_cross_attention(hx, hm, w_q, w_k, w_v, w_o):
    b, t, _ = hx.shape
    m = hm.shape[1]
    q = (hx @ w_q).reshape(b, t, MEM_HEADS, HEAD_DIM)
    k = (hm @ w_k).reshape(b, m, MEM_HEADS, HEAD_DIM)
    v = (hm @ w_v).reshape(b, m, MEM_HEADS, HEAD_DIM)
    s = jnp.einsum('bthd,bmhd->bhtm', q, k).astype(F32) * HEAD_DIM ** -0.5
    p = jax.nn.softmax(s, axis=-1).astype(v.dtype)
    o = jnp.einsum('bhtm,bmhd->bthd', p, v).reshape(b, t, MEM_WIDTH)
    return o @ w_o


def setup_inputs(seed: int = 0) -> dict:
    key = jax.random.key(seed)
    ks = jax.random.split(key, 24)
    nrm = lambda kk, shape, fan_in: jax.random.normal(kk, shape, F32) * fan_in ** -0.5
    gain = lambda kk, n: 1.0 + 0.02 * jax.random.normal(kk, (DEPTH, n), F32)
    dt = jnp.exp(jax.random.uniform(ks[5], (DEPTH, GDN_HEADS), F32, np.log(1e-3), np.log(1e-1)))
    return {
        'x': jax.random.normal(ks[0], (BATCH, SEQ, D_MODEL), F32),
        'mem': jax.random.normal(ks[1], (BATCH, MEM_LEN, D_MODEL), F32),
        'pre_mix_norm': gain(ks[2], D_MODEL),
        'w_in': nrm(ks[3], (DEPTH, D_MODEL, IN_WIDTH), D_MODEL),
        'conv_w': nrm(ks[4], (DEPTH, CONV_WIDTH, 3 * GDN_WIDTH), CONV_WIDTH),
        'a_log': jnp.log(jax.random.uniform(ks[6], (DEPTH, GDN_HEADS), F32, 1.0, 16.0)),
        'dt_bias': dt + jnp.log(-jnp.expm1(-dt)),
        'gdn_norm_w': gain(ks[7], HEAD_DIM),
        'w_out': nrm(ks[8], (DEPTH, MIX_WIDTH, D_MODEL), MIX_WIDTH),
        'post_mix_norm': gain(ks[9], D_MODEL),
        'pre_mem_norm': gain(ks[10], D_MODEL),
        'mem_kv_norm': gain(ks[11], D_MODEL),
        'w_mq': nrm(ks[12], (DEPTH, D_MODEL, MEM_WIDTH), D_MODEL),
        'w_mk': nrm(ks[13], (DEPTH, D_MODEL, MEM_WIDTH), D_MODEL),
        'w_mv': nrm(ks[14], (DEPTH, D_MODEL, MEM_WIDTH), D_MODEL),
        'w_mo': nrm(ks[15], (DEPTH, MEM_WIDTH, D_MODEL), MEM_WIDTH),
        'post_mem_norm': gain(ks[16], D_MODEL),
        'pre_mlp_norm': gain(ks[17], D_MODEL),
        'w_up': nrm(ks[18], (DEPTH, D_MODEL, D_FF), D_MODEL),
        'w_down': nrm(ks[19], (DEPTH, D_FF, D_MODEL), D_FF),
        'post_mlp_norm': gain(ks[20], D_MODEL),
    }


def reference(x, mem, pre_mix_norm, w_in, conv_w, a_log, dt_bias, gdn_norm_w, w_out, post_mix_norm,
              pre_mem_norm, mem_kv_norm, w_mq, w_mk, w_mv, w_mo, post_mem_norm,
              pre_mlp_norm, w_up, w_down, post_mlp_norm):
    split_at = [int(s) for s in np.cumsum(IN_SPLITS)[:-1]]
    for l in range(DEPTH):
        h = rms_norm(x, pre_mix_norm[l])
        gq, gk, gv, gz, ga, gb, mq, mk, mv = jnp.split(h @ w_in[l], split_at, axis=-1)
        y_gdn = gdn_branch(gq, gk, gv, gz, ga, gb, conv_w[l], a_log[l], dt_bias[l], gdn_norm_w[l])
        y_moba = moba_branch(mq, mk, mv)
        mix = jnp.concatenate([y_gdn, y_moba], axis=-1) @ w_out[l]
        x = x + rms_norm(mix, post_mix_norm[l])
        ca = memory_cross_attention(rms_norm(x, pre_mem_norm[l]), rms_norm(mem, mem_kv_norm[l]),
                                    w_mq[l], w_mk[l], w_mv[l], w_mo[l])
        x = x + rms_norm(ca, post_mem_norm[l])
        f = jnp.square(jax.nn.relu(rms_norm(x, pre_mlp_norm[l]) @ w_up[l])) @ w_down[l]
        x = x + rms_norm(f, post_mlp_norm[l])
    return x
```

```python
import functools

import jax
import jax.numpy as jnp
from jax import lax
from jax.experimental import pallas as pl
from jax.experimental.pallas import tpu as pltpu

F32 = jnp.float32
BF16 = jnp.bfloat16
HIGHEST = lax.Precision.HIGHEST

HEAD_DIM = 128
GDN_HEADS = 8
MOBA_HEADS = 8
CONV_WIDTH = 4
GDN_CHUNK = 64
GDN_SUB = 16
MOBA_BLOCK = 256
MOBA_TOPK = 3
MEM_HEADS = 4
NORM_EPS = 1e-6

V7X_VMEM_BYTES = 64 * 1024 * 1024
VMEM_LIMIT = V7X_VMEM_BYTES - 8 * 1024 * 1024

NEG_INF = float("-inf")


def _params(semantics):
    return pltpu.CompilerParams(dimension_semantics=semantics, vmem_limit_bytes=VMEM_LIMIT)


def _dot(a, b):
    return jnp.dot(a, b, preferred_element_type=F32)


def _dot_nt(a, b):
    return lax.dot_general(a, b, (((1,), (1,)), ((), ())), preferred_element_type=F32)


def _dot_tn(a, b):
    return lax.dot_general(a, b, (((0,), (0,)), ((), ())), preferred_element_type=F32)


def _rms(y, gain):
    return y * lax.rsqrt(jnp.mean(y * y, axis=-1, keepdims=True) + NORM_EPS) * gain


def _norm_matmul_kernel(x_ref, g_ref, w_ref, cs_ref, o_ref, hn_ref, *, relu2):
    @pl.when(pl.program_id(1) == 0)
    def _():
        hn_ref[...] = _rms(x_ref[...], g_ref[...]).astype(BF16)

    y = _dot(hn_ref[...], w_ref[...]) * cs_ref[...]
    if relu2:
        y = jnp.square(jnp.maximum(y, 0.0))
    o_ref[...] = y.astype(o_ref.dtype)


def _norm_matmul(x2d, gain, w_bf16, col_scale, out_dtype, *, tm, tn, relu2=False, name):
    m, d = x2d.shape
    n = w_bf16.shape[1]
    tm, tn = min(tm, m), min(tn, n)
    return pl.pallas_call(
        functools.partial(_norm_matmul_kernel, relu2=relu2),
        grid=(m // tm, n // tn),
        in_specs=[
            pl.BlockSpec((tm, d), lambda i, j: (i, 0)),
            pl.BlockSpec((1, d), lambda i, j: (0, 0)),
            pl.BlockSpec((d, tn), lambda i, j: (0, j)),
            pl.BlockSpec((1, tn), lambda i, j: (0, j)),
        ],
        out_specs=pl.BlockSpec((tm, tn), lambda i, j: (i, j)),
        out_shape=jax.ShapeDtypeStruct((m, n), out_dtype),
        scratch_shapes=[pltpu.VMEM((tm, d), BF16)],
        compiler_params=_params(("parallel", "arbitrary")),
        name=name,
    )(x2d, gain.reshape(1, d), w_bf16, col_scale.reshape(1, n))


def _matmul_norm_res_kernel(*refs, n_in):
    a_refs = refs[:n_in]
    w_ref, g_ref, r_ref, o_ref, acc_ref = refs[n_in:]
    k = pl.program_id(1)

    @pl.when(k == 0)
    def _():
        acc_ref[...] = jnp.zeros_like(acc_ref)

    off = 0
    acc = acc_ref[...]
    for a_ref in a_refs:
        width = a_ref.shape[1]
        acc = acc + _dot(a_ref[...], w_ref[off:off + width, :])
        off += width
    acc_ref[...] = acc

    @pl.when(k == pl.num_programs(1) - 1)
    def _():
        o_ref[...] = r_ref[...] + _rms(acc_ref[...], g_ref[...])


def _matmul_norm_res(a_list, w_bf16, gain, resid, *, tm, tk, name):
    m, d = resid.shape
    ktot = w_bf16.shape[0]
    n_in = len(a_list)
    tm = min(tm, m)
    if n_in > 1:
        tk = ktot
        a_specs = [pl.BlockSpec((tm, a.shape[1]), lambda i, k: (i, 0)) for a in a_list]
    else:
        tk = min(tk, ktot)
        a_specs = [pl.BlockSpec((tm, tk), lambda i, k: (i, k))]
    return pl.pallas_call(
        functools.partial(_matmul_norm_res_kernel, n_in=n_in),
        grid=(m // tm, ktot // tk),
        in_specs=a_specs + [
            pl.BlockSpec((tk, d), lambda i, k: (k, 0)),
            pl.BlockSpec((1, d), lambda i, k: (0, 0)),
            pl.BlockSpec((tm, d), lambda i, k: (i, 0)),
        ],
        out_specs=pl.BlockSpec((tm, d), lambda i, k: (i, 0)),
        out_shape=jax.ShapeDtypeStruct((m, d), F32),
        scratch_shapes=[pltpu.VMEM((tm, d), F32)],
        compiler_params=_params(("parallel", "arbitrary")),
        name=name,
    )(*a_list, w_bf16, gain.reshape(1, d), resid)


def _gdn_kernel(q_ref, k_ref, v_ref, z_ref, ab_ref, cwq_ref, cwk_ref, cwv_ref, alog_ref, dtb_ref, nw_ref,
                o_ref, state_ref, tail_ref, pad_ref, qs_ref, ks_ref, vs_ref, gs_ref, bs_ref, os_ref):
    tt = q_ref.shape[0]
    n_chunks = tt // GDN_CHUNK
    h = pl.program_id(1)

    @pl.when(pl.program_id(2) == 0)
    def _():
        state_ref[...] = jnp.zeros_like(state_ref)
        tail_ref[...] = jnp.zeros_like(tail_ref)

    def conv_silu(x_ref, slot, cw_ref):
        pad_ref[0:8, :] = tail_ref[slot]
        pad_ref[8:, :] = x_ref[...]
        tail_ref[slot] = x_ref[tt - 8:tt, :]
        y = cw_ref[0:1, :] * pad_ref[pl.ds(8 - (CONV_WIDTH - 1), tt), :]
        for j in range(1, CONV_WIDTH):
            y = y + cw_ref[j:j + 1, :] * pad_ref[pl.ds(8 - (CONV_WIDTH - 1) + j, tt), :]
        return y * jax.nn.sigmoid(y)

    def l2n(y):
        return y * lax.rsqrt(jnp.sum(y * y, axis=-1, keepdims=True) + NORM_EPS)

    qs_ref[...] = l2n(conv_silu(q_ref, 0, cwq_ref)) * (HEAD_DIM ** -0.5)
    ks_ref[...] = l2n(conv_silu(k_ref, 1, cwk_ref))
    vs_ref[...] = conv_silu(v_ref, 2, cwv_ref)

    ab = ab_ref[...]
    lane_in = lax.broadcasted_iota(jnp.int32, (HEAD_DIM, HEAD_DIM), 0)
    pick_a = (lane_in == h).astype(F32)
    pick_b = (lane_in == h + GDN_HEADS).astype(F32)
    a_b = jnp.dot(ab, pick_a, precision=HIGHEST, preferred_element_type=F32)
    b_b = jnp.dot(ab, pick_b, precision=HIGHEST, preferred_element_type=F32)
    xa = a_b + dtb_ref[...]
    softplus = jnp.maximum(xa, 0.0) + jnp.log(1.0 + jnp.exp(-jnp.abs(xa)))
    gs_ref[...] = -(jnp.exp(alog_ref[...]) * softplus)
    bs_ref[...] = jax.nn.sigmoid(b_b)

    c = GDN_CHUNK
    row = lax.broadcasted_iota(jnp.int32, (c, c), 0)
    col = lax.broadcasted_iota(jnp.int32, (c, c), 1)
    causal = row >= col
    strict = row > col
    ltri = causal.astype(F32)
    ustrict = (row > col).astype(F32)
    same_sub = (row // GDN_SUB) == (col // GDN_SUB)
    gain = nw_ref[...]

    def bdot(a, b):
        return _dot(a.astype(BF16), b.astype(BF16))

    def chunk_step(ci, carry):
        r0 = pl.multiple_of(ci * c, c)
        q = qs_ref[pl.ds(r0, c), :]
        k = ks_ref[pl.ds(r0, c), :]
        v = vs_ref[pl.ds(r0, c), :]
        g = gs_ref[pl.ds(r0, c), :]
        beta = bs_ref[pl.ds(r0, c), :]
        gc = jnp.dot(ltri, g, precision=HIGHEST, preferred_element_type=F32)
        dexp = jnp.dot(ltri, g[:, :c] * ustrict, precision=HIGHEST, preferred_element_type=F32)
        decay = jnp.where(causal, jnp.exp(dexp), 0.0)
        eg = jnp.exp(gc)
        gc_last = gc[c - 1:c, :]
        kb = k * beta
        kbf = k.astype(BF16)
        mm = jnp.where(strict, _dot_nt(kb.astype(BF16), kbf) * decay, 0.0)
        x = jnp.where(same_sub, -mm, 0.0)
        lo = jnp.where(same_sub, 0.0, mm)
        x2 = bdot(x, x)
        x4 = bdot(x2, x2)
        x8 = bdot(x4, x4)
        a1 = x + x2 + bdot(x, x2)
        a2 = a1 + x4 + bdot(a1, x4)
        dm = a2 + x8 + bdot(a2, x8)
        y = -(lo + bdot(dm, lo))
        y2 = bdot(y, y)
        qm = y + y2 + bdot(y, y2)
        tm = qm + dm + bdot(qm, dm)
        vb = v * beta
        kbg = kb * eg
        u = vb + bdot(tm, vb)
        w = kbg + bdot(tm, kbg)
        a_qk = _dot_nt(q.astype(BF16), kbf) * decay
        qg = q * eg
        kt = k * jnp.exp(gc_last - gc)
        s = state_ref[...]
        sb = s.astype(BF16)
        v_new = u - _dot(w.astype(BF16), sb)
        o = _dot(qg.astype(BF16), sb) + bdot(a_qk, v_new)
        state_ref[...] = s * jnp.exp(gc_last) + _dot_tn(kt.astype(BF16), v_new.astype(BF16))
        os_ref[pl.ds(r0, c), :] = o
        return carry

    lax.fori_loop(0, n_chunks, chunk_step, 0)

    z = z_ref[...]
    o_ref[...] = (_rms(os_ref[...], gain) * (z * jax.nn.sigmoid(z))).astype(o_ref.dtype)


def _gdn(proj_g, proj_ab, conv_w, a_log, dt_bias, norm_w, batch, seq, *, tt):
    m = proj_g.shape[0]
    nt = seq // tt
    hd = HEAD_DIM
    row = lambda b, h, t: b * nt + t
    lane_b = lambda p: jnp.broadcast_to(p.astype(F32)[:, None, None], (GDN_HEADS, 1, hd))
    tile = lambda off: pl.BlockSpec((tt, hd), lambda b, h, t: (row(b, h, t), off + h))
    cw = lambda off: pl.BlockSpec((CONV_WIDTH, hd), lambda b, h, t: (0, off + h))
    per_head = pl.BlockSpec((None, 1, hd), lambda b, h, t: (h, 0, 0))
    return pl.pallas_call(
        _gdn_kernel,
        grid=(batch, GDN_HEADS, nt),
        in_specs=[
            tile(0), tile(GDN_HEADS), tile(2 * GDN_HEADS), tile(3 * GDN_HEADS),
            pl.BlockSpec((tt, hd), lambda b, h, t: (row(b, h, t), 0)),
            cw(0), cw(GDN_HEADS), cw(2 * GDN_HEADS),
            per_head, per_head,
            pl.BlockSpec((1, hd), lambda b, h, t: (0, 0)),
        ],
        out_specs=pl.BlockSpec((tt, hd), lambda b, h, t: (row(b, h, t), h)),
        out_shape=jax.ShapeDtypeStruct((m, GDN_HEADS * hd), BF16),
        scratch_shapes=[
            pltpu.VMEM((hd, hd), F32),
            pltpu.VMEM((3, 8, hd), F32),
            pltpu.VMEM((tt + 8, hd), F32),
            pltpu.VMEM((tt, hd), F32),
            pltpu.VMEM((tt, hd), F32),
            pltpu.VMEM((tt, hd), F32),
            pltpu.VMEM((tt, hd), F32),
            pltpu.VMEM((tt, hd), F32),
            pltpu.VMEM((tt, hd), F32),
        ],
        compiler_params=_params(("parallel", "parallel", "arbitrary")),
        name="gdn",
    )(proj_g, proj_g, proj_g, proj_g, proj_ab, conv_w, conv_w, conv_w,
      lane_b(a_log), lane_b(dt_bias), norm_w.reshape(1, hd))


def _moba_kernel(q_ref, k_ref, v_ref, slope_ref, o_ref, kmean_ref, sel_ref, bias_ref):
    blk = MOBA_BLOCK
    nb = k_ref.shape[0] // blk
    i = pl.program_id(2)
    slope = slope_ref[...]
    key_pos = lax.broadcasted_iota(jnp.int32, (blk, blk), 0)
    qry_pos = lax.broadcasted_iota(jnp.int32, (blk, blk), 1)

    @pl.when(i == 0)
    def _():
        def mean_step(j, carry):
            r0 = pl.multiple_of(j * blk, blk)
            kmean_ref[pl.ds(j, 1), :] = jnp.mean(k_ref[pl.ds(r0, blk), :].astype(F32), axis=0, keepdims=True)
            return carry
        lax.fori_loop(0, nb, mean_step, 0)
        bias_ref[...] = -(slope * (qry_pos - key_pos).astype(F32))

    q = q_ref[...]
    gate = lax.dot_general(kmean_ref[...], q.astype(F32), (((1,), (1,)), ((), ())),
                           precision=HIGHEST, preferred_element_type=F32)
    blk_id = lax.broadcasted_iota(jnp.int32, (nb, blk), 0)
    gate = jnp.where(blk_id < i, gate, NEG_INF)
    sel = jnp.zeros((nb, blk), F32)
    for _ in range(MOBA_TOPK):
        best = jnp.max(gate, axis=0, keepdims=True)
        hit = (gate == best) & (best > NEG_INF)
        first = jnp.min(jnp.where(hit, blk_id, nb), axis=0, keepdims=True)
        pick = blk_id == first
        sel = jnp.where(pick, 1.0, sel)
        gate = jnp.where(pick, NEG_INF, gate)
    sel_ref[...] = sel

    r0 = pl.multiple_of(i * blk, blk)
    s = _dot_nt(k_ref[pl.ds(r0, blk), :], q) + bias_ref[...]
    s = jnp.where(key_pos <= qry_pos, s, NEG_INF)
    m0 = jnp.max(s, axis=0, keepdims=True)
    p = jnp.exp(s - m0)
    l0 = jnp.sum(p, axis=0, keepdims=True)
    acc0 = _dot_tn(v_ref[pl.ds(r0, blk), :], p.astype(BF16))

    def past_block(j, carry):
        m, l, acc = carry
        c0 = pl.multiple_of(j * blk, blk)
        chosen = sel_ref[pl.ds(j, 1), :] > 0.0
        off = slope * ((i - j) * blk).astype(F32)
        s1 = _dot_nt(k_ref[pl.ds(c0, blk), :], q) + bias_ref[...]
        mb = jnp.where(chosen, jnp.max(s1, axis=0, keepdims=True) - off, NEG_INF)
        m_new = jnp.maximum(m, mb)
        alpha = jnp.exp(m - m_new)
        p1 = jnp.exp(s1 - jnp.where(chosen, m_new + off, jnp.inf))
        l = alpha * l + jnp.sum(p1, axis=0, keepdims=True)
        acc = acc * alpha + _dot_tn(v_ref[pl.ds(c0, blk), :], p1.astype(BF16))
        return m_new, l, acc

    _, l, acc = lax.fori_loop(0, i, past_block, (m0, l0, acc0))
    o_ref[...] = (acc / l).T.astype(o_ref.dtype)


def _moba(proj_m, slopes, batch, seq):
    m = proj_m.shape[0]
    blk, hd = MOBA_BLOCK, HEAD_DIM
    nb = seq // blk
    slope_b = jnp.broadcast_to(slopes.astype(F32)[:, None, None], (MOBA_HEADS, 1, blk))
    return pl.pallas_call(
        _moba_kernel,
        grid=(batch, MOBA_HEADS, nb),
        in_specs=[
            pl.BlockSpec((blk, hd), lambda b, h, i: (b * nb + i, h)),
            pl.BlockSpec((seq, hd), lambda b, h, i: (b, MOBA_HEADS + h)),
            pl.BlockSpec((seq, hd), lambda b, h, i: (b, 2 * MOBA_HEADS + h)),
            pl.BlockSpec((None, 1, blk), lambda b, h, i: (h, 0, 0)),
        ],
        out_specs=pl.BlockSpec((blk, hd), lambda b, h, i: (b * nb + i, h)),
        out_shape=jax.ShapeDtypeStruct((m, MOBA_HEADS * hd), BF16),
        scratch_shapes=[
            pltpu.VMEM((nb, hd), F32),
            pltpu.VMEM((nb, blk), F32),
            pltpu.VMEM((blk, blk), F32),
        ],
        compiler_params=_params(("parallel", "parallel", "arbitrary")),
        name="moba",
    )(proj_m, proj_m, proj_m, slope_b)


def _mem_attn_kernel(x_ref, gpre_ref, wq_ref, kv_ref, wo_ref, gpost_ref, o_ref):
    hd = HEAD_DIM
    width = MEM_HEADS * hd
    x = x_ref[...]
    hn = _rms(x, gpre_ref[...]).astype(BF16)
    q = (_dot(hn, wq_ref[...]) * (hd ** -0.5)).astype(BF16)
    kv = kv_ref[...]
    heads = []
    for hh in range(MEM_HEADS):
        s = _dot_nt(q[:, hh * hd:(hh + 1) * hd], kv[:, hh * hd:(hh + 1) * hd])
        p = jnp.exp(s - jnp.max(s, axis=-1, keepdims=True))
        p = p / jnp.sum(p, axis=-1, keepdims=True)
        heads.append(_dot(p.astype(BF16), kv[:, width + hh * hd:width + (hh + 1) * hd]))
    o = jnp.concatenate(heads, axis=-1).astype(BF16)
    o_ref[...] = x + _rms(_dot(o, wo_ref[...]), gpost_ref[...])


def _mem_attn(x2d, g_pre, wq_bf16, kv_bf16, wo_bf16, g_post, batch, seq, *, tm):
    m, d = x2d.shape
    nt = seq // tm
    mem_len, kvw = kv_bf16.shape[1], kv_bf16.shape[2]
    width = wq_bf16.shape[1]
    return pl.pallas_call(
        _mem_attn_kernel,
        grid=(batch, nt),
        in_specs=[
            pl.BlockSpec((tm, d), lambda b, t: (b * nt + t, 0)),
            pl.BlockSpec((1, d), lambda b, t: (0, 0)),
            pl.BlockSpec((d, width), lambda b, t: (0, 0)),
            pl.BlockSpec((None, mem_len, kvw), lambda b, t: (b, 0, 0)),
            pl.BlockSpec((width, d), lambda b, t: (0, 0)),
            pl.BlockSpec((1, d), lambda b, t: (0, 0)),
        ],
        out_specs=pl.BlockSpec((tm, d), lambda b, t: (b * nt + t, 0)),
        out_shape=jax.ShapeDtypeStruct((m, d), F32),
        compiler_params=_params(("parallel", "parallel")),
        name="mem_attn",
    )(x2d, g_pre.reshape(1, d), wq_bf16, kv_bf16, wo_bf16, g_post.reshape(1, d))


def kernel(x, mem, pre_mix_norm, w_in, conv_w, a_log, dt_bias, gdn_norm_w, w_out, post_mix_norm,
           pre_mem_norm, mem_kv_norm, w_mq, w_mk, w_mv, w_mo, post_mem_norm,
           pre_mlp_norm, w_up, w_down, post_mlp_norm):
    batch, seq, d = x.shape
    mem_len = mem.shape[1]
    gw = GDN_HEADS * HEAD_DIM
    mw = MOBA_HEADS * HEAD_DIM
    assert seq % MOBA_BLOCK == 0 and seq % 512 == 0
    x2d = x.reshape(batch * seq, d)
    mem2d = mem.reshape(batch * mem_len, d)
    slopes = jnp.exp2(-8.0 * jnp.arange(1, MOBA_HEADS + 1, dtype=F32) / MOBA_HEADS)
    ones = lambda n: jnp.ones((n,), F32)

    for l in range(w_in.shape[0]):
        win = w_in[l]
        w_g = win[:, :4 * gw].astype(BF16)
        w_ab = jnp.pad(win[:, 4 * gw:4 * gw + 2 * GDN_HEADS], ((0, 0), (0, HEAD_DIM - 2 * GDN_HEADS))).astype(BF16)
        w_m = win[:, 4 * gw + 2 * GDN_HEADS:].astype(BF16)
        q_scale = jnp.concatenate([jnp.full((mw,), HEAD_DIM ** -0.5, F32), ones(2 * mw)])

        proj_g = _norm_matmul(x2d, pre_mix_norm[l], w_g, ones(4 * gw), F32, tm=1024, tn=1024, name="in_proj_gdn")
        proj_ab = _norm_matmul(x2d, pre_mix_norm[l], w_ab, ones(HEAD_DIM), F32, tm=1024, tn=128, name="in_proj_ab")
        proj_m = _norm_matmul(x2d, pre_mix_norm[l], w_m, q_scale, BF16, tm=1024, tn=1024, name="in_proj_moba")

        y_gdn = _gdn(proj_g, proj_ab, conv_w[l], a_log[l], dt_bias[l], gdn_norm_w[l], batch, seq, tt=512)
        y_moba = _moba(proj_m, slopes, batch, seq)
        x2d = _matmul_norm_res([y_gdn, y_moba], w_out[l].astype(BF16), post_mix_norm[l], x2d,
                               tm=512, tk=gw + mw, name="out_proj")

        w_kv = jnp.concatenate([w_mk[l], w_mv[l]], axis=1).astype(BF16)
        kv = _norm_matmul(mem2d, mem_kv_norm[l], w_kv, ones(w_kv.shape[1]), BF16, tm=512, tn=1024, name="mem_kv")
        x2d = _mem_attn(x2d, pre_mem_norm[l], w_mq[l].astype(BF16), kv.reshape(batch, mem_len, -1),
                        w_mo[l].astype(BF16), post_mem_norm[l], batch, seq, tm=512)

        hid = _norm_matmul(x2d, pre_mlp_norm[l], w_up[l].astype(BF16), ones(w_up.shape[2]), BF16,
                           tm=1024, tn=1024, relu2=True, name="mlp_up")
        x2d = _matmul_norm_res([hid], w_down[l].astype(BF16), post_mlp_norm[l], x2d, tm=512, tk=2048, name="mlp_down")
    return x2d.reshape(batch, seq, d)
```

```python
import functools
import math

import jax
import jax.numpy as jnp
from jax import lax
from jax.experimental import pallas as pl
from jax.experimental.pallas import tpu as pltpu

F32 = jnp.float32
BF16 = jnp.bfloat16
HIGHEST = lax.Precision.HIGHEST

HEAD_DIM = 128
GDN_HEADS = 8
MOBA_HEADS = 8
CONV_WIDTH = 4
GDN_CHUNK = 64
GDN_SUB = 16
MOBA_BLOCK = 256
MOBA_TOPK = 3
MOBA_GROUP = 4
MEM_HEADS = 4
NORM_EPS = 1e-6
LOG2E = math.log2(math.e)

V7X_VMEM_BYTES = 64 * 1024 * 1024
VMEM_LIMIT = V7X_VMEM_BYTES - 8 * 1024 * 1024

NEG_INF = float("-inf")


def _params(semantics):
    return pltpu.CompilerParams(dimension_semantics=semantics, vmem_limit_bytes=VMEM_LIMIT)


def _dot(a, b):
    return jnp.dot(a, b, preferred_element_type=F32)


def _dot_nt(a, b):
    return lax.dot_general(a, b, (((1,), (1,)), ((), ())), preferred_element_type=F32)


def _dot_tn(a, b):
    return lax.dot_general(a, b, (((0,), (0,)), ((), ())), preferred_element_type=F32)


def _bdot(a, b):
    return _dot(a.astype(BF16), b.astype(BF16))


def _split3(x):
    hi = x.astype(BF16)
    rest = x - hi.astype(F32)
    mid = rest.astype(BF16)
    lo = (rest - mid.astype(F32)).astype(BF16)
    return hi, mid, lo


def _rms(y, gain):
    return y * lax.rsqrt(jnp.mean(y * y, axis=-1, keepdims=True) + NORM_EPS) * gain


def _norm_matmul_kernel(x_ref, g_ref, w_ref, cs_ref, o_ref, hn_ref, *, relu2):
    @pl.when(pl.program_id(1) == 0)
    def _():
        hn_ref[...] = _rms(x_ref[...], g_ref[...]).astype(BF16)

    y = _dot(hn_ref[...], w_ref[...]) * cs_ref[...]
    if relu2:
        y = jnp.square(jnp.maximum(y, 0.0))
    o_ref[...] = y.astype(o_ref.dtype)


def _norm_matmul(x2d, gain, w_bf16, col_scale, out_dtype, *, tm, tn, relu2=False, name):
    m, d = x2d.shape
    n = w_bf16.shape[1]
    tm, tn = min(tm, m), min(tn, n)
    return pl.pallas_call(
        functools.partial(_norm_matmul_kernel, relu2=relu2),
        grid=(m // tm, n // tn),
        in_specs=[
            pl.BlockSpec((tm, d), lambda i, j: (i, 0)),
            pl.BlockSpec((1, d), lambda i, j: (0, 0)),
            pl.BlockSpec((d, tn), lambda i, j: (0, j)),
            pl.BlockSpec((1, tn), lambda i, j: (0, j)),
        ],
        out_specs=pl.BlockSpec((tm, tn), lambda i, j: (i, j)),
        out_shape=jax.ShapeDtypeStruct((m, n), out_dtype),
        scratch_shapes=[pltpu.VMEM((tm, d), BF16)],
        compiler_params=_params(("parallel", "arbitrary")),
        name=name,
    )(x2d, gain.reshape(1, d), w_bf16, col_scale.reshape(1, n))


def _matmul_norm_res_kernel(*refs, n_in):
    a_refs = refs[:n_in]
    w_ref, g_ref, r_ref, o_ref, acc_ref = refs[n_in:]
    k = pl.program_id(1)

    @pl.when(k == 0)
    def _():
        acc_ref[...] = jnp.zeros_like(acc_ref)

    off = 0
    acc = acc_ref[...]
    for a_ref in a_refs:
        width = a_ref.shape[1]
        acc = acc + _dot(a_ref[...], w_ref[off:off + width, :])
        off += width
    acc_ref[...] = acc

    @pl.when(k == pl.num_programs(1) - 1)
    def _():
        o_ref[...] = r_ref[...] + _rms(acc_ref[...], g_ref[...])


def _matmul_norm_res(a_list, w_bf16, gain, resid, *, tm, tk, name):
    m, d = resid.shape
    ktot = w_bf16.shape[0]
    n_in = len(a_list)
    tm = min(tm, m)
    if n_in > 1:
        tk = ktot
        a_specs = [pl.BlockSpec((tm, a.shape[1]), lambda i, k: (i, 0)) for a in a_list]
    else:
        tk = min(tk, ktot)
        a_specs = [pl.BlockSpec((tm, tk), lambda i, k: (i, k))]
    return pl.pallas_call(
        functools.partial(_matmul_norm_res_kernel, n_in=n_in),
        grid=(m // tm, ktot // tk),
        in_specs=a_specs + [
            pl.BlockSpec((tk, d), lambda i, k: (k, 0)),
            pl.BlockSpec((1, d), lambda i, k: (0, 0)),
            pl.BlockSpec((tm, d), lambda i, k: (i, 0)),
        ],
        out_specs=pl.BlockSpec((tm, d), lambda i, k: (i, 0)),
        out_shape=jax.ShapeDtypeStruct((m, d), F32),
        scratch_shapes=[pltpu.VMEM((tm, d), F32)],
        compiler_params=_params(("parallel", "arbitrary")),
        name=name,
    )(*a_list, w_bf16, gain.reshape(1, d), resid)


def _gdn_kernel(q_ref, k_ref, v_ref, z_ref, ab_ref, cwq_ref, cwk_ref, cwv_ref, alog_ref, dtb_ref, nw_ref,
                o_ref, state_ref, tail_ref, pad_ref, qs_ref, ks_ref, vs_ref, gs_ref, bs_ref, os_ref,
                w2_ref, r_ref, qp_ref, op_ref, gl_ref):
    tt = q_ref.shape[0]
    c = GDN_CHUNK
    hd = HEAD_DIM
    hps = q_ref.shape[1] // hd
    n_chunks = tt // c
    head0 = pl.program_id(1) * hps
    lanes = lambda hh: slice(hh * hd, (hh + 1) * hd)

    @pl.when(pl.program_id(2) == 0)
    def _():
        state_ref[...] = jnp.zeros_like(state_ref)
        tail_ref[...] = jnp.zeros_like(tail_ref)

    def conv_silu(x_ref, slot, cw_ref):
        pad_ref[0:8, :] = tail_ref[slot]
        pad_ref[8:, :] = x_ref[...]
        tail_ref[slot] = x_ref[tt - 8:tt, :]
        y = cw_ref[0:1, :] * pad_ref[pl.ds(8 - (CONV_WIDTH - 1), tt), :]
        for j in range(1, CONV_WIDTH):
            y = y + cw_ref[j:j + 1, :] * pad_ref[pl.ds(8 - (CONV_WIDTH - 1) + j, tt), :]
        return y * jax.nn.sigmoid(y)

    def l2n(y):
        return y * lax.rsqrt(jnp.sum(y * y, axis=-1, keepdims=True) + NORM_EPS)

    qc = conv_silu(q_ref, 0, cwq_ref)
    kc = conv_silu(k_ref, 1, cwk_ref)
    vs_ref[...] = conv_silu(v_ref, 2, cwv_ref)
    ab = ab_ref[...]
    lane = lax.broadcasted_iota(jnp.int32, ab.shape, 1)
    for hh in range(hps):
        qs_ref[:, lanes(hh)] = l2n(qc[:, lanes(hh)]) * (HEAD_DIM ** -0.5)
        ks_ref[:, lanes(hh)] = l2n(kc[:, lanes(hh)])
        a_col = jnp.sum(jnp.where(lane == head0 + hh, ab, 0.0), axis=-1, keepdims=True)
        b_col = jnp.sum(jnp.where(lane == head0 + hh + GDN_HEADS, ab, 0.0), axis=-1, keepdims=True)
        xa = a_col + dtb_ref[hh]
        softplus = jnp.maximum(xa, 0.0) + jnp.log(1.0 + jnp.exp(-jnp.abs(xa)))
        gs_ref[:, lanes(hh)] = -(jnp.exp(alog_ref[hh]) * softplus)
        bs_ref[:, lanes(hh)] = jnp.broadcast_to(jax.nn.sigmoid(b_col), (tt, hd))

    row = lax.broadcasted_iota(jnp.int32, (c, c), 0)
    col = lax.broadcasted_iota(jnp.int32, (c, c), 1)
    causal = row >= col
    strict = row > col
    same_sub = (row // GDN_SUB) == (col // GDN_SUB)
    ltri3 = jnp.concatenate([causal.astype(BF16)] * 3, axis=1)

    units = [(n, hh) for n in range(n_chunks) for hh in range(hps)]
    each = lambda fn, *cols: [fn(*args) for args in zip(*cols)]
    rows = lambda ref: [ref[n * c:(n + 1) * c, lanes(hh)] for n, hh in units]
    q, k, v, g, beta = rows(qs_ref), rows(ks_ref), rows(vs_ref), rows(gs_ref), rows(bs_ref)
    pieces = each(_split3, g)
    gc = each(lambda p: _dot(ltri3, jnp.concatenate(p, axis=0)), pieces)
    dexp = each(lambda p: _dot(ltri3, jnp.concatenate([jnp.where(strict, x[:, :c], 0.0) for x in p], axis=0)), pieces)
    decay = each(lambda d: jnp.where(causal, jnp.exp(d), 0.0), dexp)
    kb = each(lambda a, b: a * b, k, beta)
    kbf = each(lambda a: a.astype(BF16), k)
    mm = each(lambda a, b, d: jnp.where(strict, _dot_nt(a.astype(BF16), b) * d, 0.0), kb, kbf, decay)
    a_qk = each(lambda a, b, d: (_dot_nt(a.astype(BF16), b) * d).astype(BF16), q, kbf, decay)
    x = each(lambda m_: jnp.where(same_sub, -m_, 0.0), mm)
    lo = each(lambda m_: jnp.where(same_sub, 0.0, m_), mm)
    x2 = each(_bdot, x, x)
    x4 = each(_bdot, x2, x2)
    a1 = each(lambda a, b: a + b + _bdot(a, b), x, x2)
    x8 = each(_bdot, x4, x4)
    a2 = each(lambda a, b: a + b + _bdot(a, b), a1, x4)
    dm = each(lambda a, b: a + b + _bdot(a, b), a2, x8)
    y = each(lambda d, l_: -(l_ + _bdot(d, l_)), dm, lo)
    y2 = each(_bdot, y, y)
    qm = each(lambda a, b: a + b + _bdot(a, b), y, y2)
    tm = each(lambda a, b: a + b + _bdot(a, b), qm, dm)
    eg = each(jnp.exp, gc)
    uw = each(lambda v_, b_, kb_, e_: jnp.concatenate([v_ * b_, kb_ * e_], axis=1), v, beta, kb, eg)
    uw = each(lambda t_, u_: (u_ + _bdot(t_, u_)).astype(BF16), tm, uw)
    au = each(_dot, a_qk, uw)
    kt = each(lambda k_, g_: (k_ * jnp.exp(g_[c - 1:c, :] - g_)).astype(BF16), k, gc)
    kw = each(_dot_tn, kt, uw)
    for u in range(len(units)):
        op_ref[u] = au[u][:, :hd]
        qp_ref[u] = (q[u] * eg[u] - au[u][:, hd:]).astype(BF16)
        r_ref[u] = kw[u][:, :hd]
        w2_ref[u] = kw[u][:, hd:].astype(BF16)
        gl_ref[u] = jnp.exp(gc[u][c - 1:c, :])

    s = [state_ref[hh] for hh in range(hps)]
    for u, (n, hh) in enumerate(units):
        sb = s[hh].astype(BF16)
        os_ref[n * c:(n + 1) * c, lanes(hh)] = op_ref[u] + _dot(qp_ref[u], sb)
        s[hh] = s[hh] * gl_ref[u] + r_ref[u] - _dot(w2_ref[u], sb)
    for hh in range(hps):
        state_ref[hh] = s[hh]
        z = z_ref[:, lanes(hh)]
        o_ref[:, lanes(hh)] = (_rms(os_ref[:, lanes(hh)], nw_ref[...]) * (z * jax.nn.sigmoid(z))).astype(o_ref.dtype)


def _gdn(proj_g, proj_ab, conv_w, a_log, dt_bias, norm_w, batch, seq, *, tt, hps):
    m = proj_g.shape[0]
    nt = seq // tt
    hd = HEAD_DIM
    wide = hps * hd
    ng = GDN_HEADS // hps
    n_units = (tt // GDN_CHUNK) * hps
    row = lambda b, hg, t: b * nt + t
    lane_b = lambda p: jnp.broadcast_to(p.astype(F32)[:, None, None], (GDN_HEADS, 1, hd))
    tile = lambda part: pl.BlockSpec((tt, wide), lambda b, hg, t: (row(b, hg, t), part * ng + hg))
    cw = lambda part: pl.BlockSpec((CONV_WIDTH, wide), lambda b, hg, t: (0, part * ng + hg))
    per_head = pl.BlockSpec((hps, 1, hd), lambda b, hg, t: (hg, 0, 0))
    return pl.pallas_call(
        _gdn_kernel,
        grid=(batch, ng, nt),
        in_specs=[
            tile(0), tile(1), tile(2), tile(3),
            pl.BlockSpec((tt, hd), lambda b, hg, t: (row(b, hg, t), 0)),
            cw(0), cw(1), cw(2),
            per_head, per_head,
            pl.BlockSpec((1, hd), lambda b, hg, t: (0, 0)),
        ],
        out_specs=pl.BlockSpec((tt, wide), lambda b, hg, t: (row(b, hg, t), hg)),
        out_shape=jax.ShapeDtypeStruct((m, GDN_HEADS * hd), BF16),
        scratch_shapes=[
            pltpu.VMEM((hps, hd, hd), F32),
            pltpu.VMEM((3, 8, wide), F32),
            pltpu.VMEM((tt + 8, wide), F32),
            pltpu.VMEM((tt, wide), F32),
            pltpu.VMEM((tt, wide), F32),
            pltpu.VMEM((tt, wide), F32),
            pltpu.VMEM((tt, wide), F32),
            pltpu.VMEM((tt, wide), F32),
            pltpu.VMEM((tt, wide), F32),
            pltpu.VMEM((n_units, hd, hd), BF16),
            pltpu.VMEM((n_units, hd, hd), F32),
            pltpu.VMEM((n_units, GDN_CHUNK, hd), BF16),
            pltpu.VMEM((n_units, GDN_CHUNK, hd), F32),
            pltpu.VMEM((n_units, 1, hd), F32),
        ],
        compiler_params=_params(("parallel", "parallel", "arbitrary")),
        name="gdn",
    )(proj_g, proj_g, proj_g, proj_g, proj_ab, conv_w, conv_w, conv_w,
      lane_b(a_log), lane_b(dt_bias), norm_w.reshape(1, hd))


def _moba_kernel(q_ref, k_ref, v_ref, slope_ref, o_ref, kaug_ref, qaug_ref, kmean_ref, sel_ref, sa_ref, sb_ref):
    blk = MOBA_BLOCK
    hd = HEAD_DIM
    nb = k_ref.shape[0] // blk
    i = pl.program_id(2)
    slope2 = slope_ref[...]

    @pl.when(i == 0)
    def _():
        pos = lax.broadcasted_iota(jnp.int32, (blk, hd), 0).astype(F32)
        lane = lax.broadcasted_iota(jnp.int32, (blk, hd), 1)
        hi, mid, lo = _split3(slope2[:, :hd] * pos)
        one = jnp.ones((blk, hd), BF16)
        zero = jnp.zeros((blk, hd), BF16)
        pieces = jnp.where(lane == 0, hi, jnp.where(lane == 1, mid, jnp.where(lane == 2, lo, zero)))
        k_extra = jnp.where((lane >= 3) & (lane < 6), one, pieces)
        npieces = jnp.where(lane == 3, -hi, jnp.where(lane == 4, -mid, jnp.where(lane == 5, -lo, zero)))
        qaug_ref[:, hd:] = jnp.where(lane < 3, one, npieces)

        def block_step(j, carry):
            r0 = pl.multiple_of(j * blk, blk)
            kb = k_ref[pl.ds(r0, blk), :]
            kaug_ref[pl.ds(r0, blk), 0:hd] = kb
            kaug_ref[pl.ds(r0, blk), hd:] = k_extra
            kmean_ref[pl.ds(j, 1), :] = jnp.mean(kb.astype(F32), axis=0, keepdims=True)
            return carry
        lax.fori_loop(0, nb, block_step, 0)

    q = q_ref[...]
    qaug_ref[:, 0:hd] = q
    qa = qaug_ref[...]
    gate = lax.dot_general(kmean_ref[...], q.astype(F32), (((1,), (1,)), ((), ())),
                           precision=HIGHEST, preferred_element_type=F32)
    blk_id = lax.broadcasted_iota(jnp.int32, (nb, blk), 0)
    gate = jnp.where(blk_id < i, gate, NEG_INF)
    sel = jnp.zeros((nb, blk), F32)
    for _ in range(MOBA_TOPK):
        best = jnp.max(gate, axis=0, keepdims=True)
        hit = (gate == best) & (best > NEG_INF)
        first = jnp.min(jnp.where(hit, blk_id, nb), axis=0, keepdims=True)
        pick = blk_id == first
        sel = jnp.where(pick, 1.0, sel)
        gate = jnp.where(pick, NEG_INF, gate)
    sel_ref[...] = sel

    key_pos = lax.broadcasted_iota(jnp.int32, (blk, blk), 0)
    qry_pos = lax.broadcasted_iota(jnp.int32, (blk, blk), 1)
    r0 = pl.multiple_of(i * blk, blk)
    s = jnp.where(key_pos <= qry_pos, _dot_nt(kaug_ref[pl.ds(r0, blk), :], qa), NEG_INF)
    m0 = jnp.max(s, axis=0, keepdims=True)
    p = jnp.exp2(s - m0)
    l0 = jnp.sum(p, axis=0, keepdims=True)
    acc0 = _dot_tn(v_ref[pl.ds(r0, blk), :], p.astype(BF16))

    def score_group(g, dst_ref):
        for b in range(MOBA_GROUP):
            c0 = pl.multiple_of((g * MOBA_GROUP + b) * blk, blk)
            dst_ref[b] = _dot_nt(kaug_ref[pl.ds(c0, blk), :], qa)

    def attend_group(g, src_ref, carry):
        m, l, acc = carry
        subs = []
        m_new = m
        for b in range(MOBA_GROUP):
            j = g * MOBA_GROUP + b
            chosen = sel_ref[pl.ds(j, 1), :] > 0.0
            off = slope2 * ((i - j) * blk).astype(F32)
            m_new = jnp.maximum(m_new, jnp.where(chosen, jnp.max(src_ref[b], axis=0, keepdims=True) - off, NEG_INF))
            subs.append((chosen, off))
        alpha = jnp.exp2(m - m_new)
        l = alpha * l
        acc = acc * alpha
        for b, (chosen, off) in enumerate(subs):
            c0 = pl.multiple_of((g * MOBA_GROUP + b) * blk, blk)
            p2 = jnp.exp2(src_ref[b] - jnp.where(chosen, m_new + off, jnp.inf))
            l = l + jnp.sum(p2, axis=0, keepdims=True)
            acc = acc + _dot_tn(v_ref[pl.ds(c0, blk), :], p2.astype(BF16))
        return m_new, l, acc

    last_group = nb // MOBA_GROUP - 1
    n_pairs = (i + 2 * MOBA_GROUP - 1) // (2 * MOBA_GROUP)

    @pl.when(n_pairs > 0)
    def _():
        score_group(0, sa_ref)

    def past_pair(gp, carry):
        g = 2 * gp
        score_group(g + 1, sb_ref)
        carry = attend_group(g, sa_ref, carry)
        score_group(jnp.minimum(g + 2, last_group), sa_ref)
        return attend_group(g + 1, sb_ref, carry)

    _, l, acc = lax.fori_loop(0, n_pairs, past_pair, (m0, l0, acc0))
    o_ref[...] = (acc / l).T.astype(o_ref.dtype)


def _moba(proj_m, slopes, batch, seq):
    m = proj_m.shape[0]
    blk, hd = MOBA_BLOCK, HEAD_DIM
    nb = seq // blk
    assert nb % (2 * MOBA_GROUP) == 0
    slope_b = jnp.broadcast_to((slopes.astype(F32) * LOG2E)[:, None, None], (MOBA_HEADS, 1, blk))
    return pl.pallas_call(
        _moba_kernel,
        grid=(batch, MOBA_HEADS, nb),
        in_specs=[
            pl.BlockSpec((blk, hd), lambda b, h, i: (b * nb + i, h)),
            pl.BlockSpec((seq, hd), lambda b, h, i: (b, MOBA_HEADS + h)),
            pl.BlockSpec((seq, hd), lambda b, h, i: (b, 2 * MOBA_HEADS + h)),
            pl.BlockSpec((None, 1, blk), lambda b, h, i: (h, 0, 0)),
        ],
        out_specs=pl.BlockSpec((blk, hd), lambda b, h, i: (b * nb + i, h)),
        out_shape=jax.ShapeDtypeStruct((m, MOBA_HEADS * hd), BF16),
        scratch_shapes=[
            pltpu.VMEM((seq, 2 * hd), BF16),
            pltpu.VMEM((blk, 2 * hd), BF16),
            pltpu.VMEM((nb, hd), F32),
            pltpu.VMEM((nb, blk), F32),
            pltpu.VMEM((MOBA_GROUP, blk, blk), F32),
            pltpu.VMEM((MOBA_GROUP, blk, blk), F32),
        ],
        compiler_params=_params(("parallel", "parallel", "arbitrary")),
        name="moba",
    )(proj_m, proj_m, proj_m, slope_b)


def _mem_attn_kernel(x_ref, gpre_ref, wq_ref, kv_ref, wo_ref, gpost_ref, o_ref):
    hd = HEAD_DIM
    width = MEM_HEADS * hd
    x = x_ref[...]
    hn = _rms(x, gpre_ref[...]).astype(BF16)
    q = (_dot(hn, wq_ref[...]) * (hd ** -0.5)).astype(BF16)
    kv = kv_ref[...]
    heads = []
    for hh in range(MEM_HEADS):
        s = _dot_nt(q[:, hh * hd:(hh + 1) * hd], kv[:, hh * hd:(hh + 1) * hd])
        p = jnp.exp(s - jnp.max(s, axis=-1, keepdims=True))
        p = p / jnp.sum(p, axis=-1, keepdims=True)
        heads.append(_dot(p.astype(BF16), kv[:, width + hh * hd:width + (hh + 1) * hd]))
    o = jnp.concatenate(heads, axis=-1).astype(BF16)
    o_ref[...] = x + _rms(_dot(o, wo_ref[...]), gpost_ref[...])


def _mem_attn(x2d, g_pre, wq_bf16, kv_bf16, wo_bf16, g_post, batch, seq, *, tm):
    m, d = x2d.shape
    nt = seq // tm
    mem_len, kvw = kv_bf16.shape[1], kv_bf16.shape[2]
    width = wq_bf16.shape[1]
    return pl.pallas_call(
        _mem_attn_kernel,
        grid=(batch, nt),
        in_specs=[
            pl.BlockSpec((tm, d), lambda b, t: (b * nt + t, 0)),
            pl.BlockSpec((1, d), lambda b, t: (0, 0)),
            pl.BlockSpec((d, width), lambda b, t: (0, 0)),
            pl.BlockSpec((None, mem_len, kvw), lambda b, t: (b, 0, 0)),
            pl.BlockSpec((width, d), lambda b, t: (0, 0)),
            pl.BlockSpec((1, d), lambda b, t: (0, 0)),
        ],
        out_specs=pl.BlockSpec((tm, d), lambda b, t: (b * nt + t, 0)),
        out_shape=jax.ShapeDtypeStruct((m, d), F32),
        compiler_params=_params(("parallel", "parallel")),
        name="mem_attn",
    )(x2d, g_pre.reshape(1, d), wq_bf16, kv_bf16, wo_bf16, g_post.reshape(1, d))


def kernel(x, mem, pre_mix_norm, w_in, conv_w, a_log, dt_bias, gdn_norm_w, w_out, post_mix_norm,
           pre_mem_norm, mem_kv_norm, w_mq, w_mk, w_mv, w_mo, post_mem_norm,
           pre_mlp_norm, w_up, w_down, post_mlp_norm):
    batch, seq, d = x.shape
    mem_len = mem.shape[1]
    gw = GDN_HEADS * HEAD_DIM
    mw = MOBA_HEADS * HEAD_DIM
    assert seq % MOBA_BLOCK == 0 and seq % 512 == 0
    x2d = x.reshape(batch * seq, d)
    mem2d = mem.reshape(batch * mem_len, d)
    slopes = jnp.exp2(-8.0 * jnp.arange(1, MOBA_HEADS + 1, dtype=F32) / MOBA_HEADS)
    ones = lambda n: jnp.ones((n,), F32)

    for l in range(w_in.shape[0]):
        win = w_in[l]
        w_g = win[:, :4 * gw].astype(BF16)
        w_ab = jnp.pad(win[:, 4 * gw:4 * gw + 2 * GDN_HEADS], ((0, 0), (0, HEAD_DIM - 2 * GDN_HEADS))).astype(BF16)
        w_m = win[:, 4 * gw + 2 * GDN_HEADS:].astype(BF16)
        q_scale = jnp.concatenate([jnp.full((mw,), LOG2E * HEAD_DIM ** -0.5, F32), ones(2 * mw)])

        proj_g = _norm_matmul(x2d, pre_mix_norm[l], w_g, ones(4 * gw), F32, tm=1024, tn=1024, name="in_proj_gdn")
        proj_ab = _norm_matmul(x2d, pre_mix_norm[l], w_ab, ones(HEAD_DIM), F32, tm=1024, tn=128, name="in_proj_ab")
        proj_m = _norm_matmul(x2d, pre_mix_norm[l], w_m, q_scale, BF16, tm=1024, tn=1024, name="in_proj_moba")

        y_gdn = _gdn(proj_g, proj_ab, conv_w[l], a_log[l], dt_bias[l], gdn_norm_w[l], batch, seq, tt=256, hps=4)
        y_moba = _moba(proj_m, slopes, batch, seq)
        x2d = _matmul_norm_res([y_gdn, y_moba], w_out[l].astype(BF16), post_mix_norm[l], x2d,
                               tm=512, tk=gw + mw, name="out_proj")

        w_kv = jnp.concatenate([w_mk[l], w_mv[l]], axis=1).astype(BF16)
        kv = _norm_matmul(mem2d, mem_kv_norm[l], w_kv, ones(w_kv.shape[1]), BF16, tm=512, tn=1024, name="mem_kv")
        x2d = _mem_attn(x2d, pre_mem_norm[l], w_mq[l].astype(BF16), kv.reshape(batch, mem_len, -1),
                        w_mo[l].astype(BF16), post_mem_norm[l], batch, seq, tm=512)

        hid = _norm_matmul(x2d, pre_mlp_norm[l], w_up[l].astype(BF16), ones(w_up.shape[2]), BF16,
                           tm=1024, tn=1024, relu2=True, name="mlp_up")
        x2d = _matmul_norm_res([hid], w_down[l].astype(BF16), post_mlp_norm[l], x2d, tm=512, tk=2048, name="mlp_down")
    return x2d.reshape(batch, seq, d)
```

```python
import functools
import math

import jax
import jax.numpy as jnp
from jax import lax
from jax.experimental import pallas as pl
from jax.experimental.pallas import tpu as pltpu

F32 = jnp.float32
BF16 = jnp.bfloat16
HIGHEST = lax.Precision.HIGHEST

HEAD_DIM = 128
GDN_HEADS = 8
MOBA_HEADS = 8
CONV_WIDTH = 4
GDN_CHUNK = 64
GDN_SUB = 16
MOBA_BLOCK = 256
MOBA_TOPK = 3
MOBA_GROUP = 4
MEM_HEADS = 4
NORM_EPS = 1e-6
LOG2E = math.log2(math.e)

V7X_VMEM_BYTES = 64 * 1024 * 1024
VMEM_LIMIT = V7X_VMEM_BYTES - 8 * 1024 * 1024

NEG_INF = float("-inf")


def _params(semantics):
    return pltpu.CompilerParams(dimension_semantics=semantics, vmem_limit_bytes=VMEM_LIMIT)


def _dot(a, b):
    return jnp.dot(a, b, preferred_element_type=F32)


def _dot_nt(a, b):
    return lax.dot_general(a, b, (((1,), (1,)), ((), ())), preferred_element_type=F32)


def _dot_tn(a, b):
    return lax.dot_general(a, b, (((0,), (0,)), ((), ())), preferred_element_type=F32)


def _bdot(a, b):
    return _dot(a.astype(BF16), b.astype(BF16))


def _split3(x):
    hi = x.astype(BF16)
    rest = x - hi.astype(F32)
    mid = rest.astype(BF16)
    lo = (rest - mid.astype(F32)).astype(BF16)
    return hi, mid, lo


def _rms(y, gain):
    return y * lax.rsqrt(jnp.mean(y * y, axis=-1, keepdims=True) + NORM_EPS) * gain


def _norm_matmul_kernel(*refs, relu2, side):
    if side:
        x_ref, g_ref, w_ref, cs_ref, ws_ref, o_ref, os_ref, hn_ref = refs
    else:
        x_ref, g_ref, w_ref, cs_ref, o_ref, hn_ref = refs

    @pl.when(pl.program_id(1) == 0)
    def _():
        hn_ref[...] = _rms(x_ref[...], g_ref[...]).astype(BF16)
        if side:
            os_ref[...] = _dot(hn_ref[...], ws_ref[...])

    y = _dot(hn_ref[...], w_ref[...]) * cs_ref[...]
    if relu2:
        y = jnp.square(jnp.maximum(y, 0.0))
    o_ref[...] = y.astype(o_ref.dtype)


def _norm_matmul(x2d, gain, w_bf16, col_scale, out_dtype, *, tm, tn, relu2=False, side_w=None, name):
    m, d = x2d.shape
    n = w_bf16.shape[1]
    tm, tn = min(tm, m), min(tn, n)
    side = side_w is not None
    in_specs = [
        pl.BlockSpec((tm, d), lambda i, j: (i, 0)),
        pl.BlockSpec((1, d), lambda i, j: (0, 0)),
        pl.BlockSpec((d, tn), lambda i, j: (0, j)),
        pl.BlockSpec((1, tn), lambda i, j: (0, j)),
    ]
    out_specs = pl.BlockSpec((tm, tn), lambda i, j: (i, j))
    out_shape = jax.ShapeDtypeStruct((m, n), out_dtype)
    args = [x2d, gain.reshape(1, d), w_bf16, col_scale.reshape(1, n)]
    if side:
        ns = side_w.shape[1]
        in_specs.append(pl.BlockSpec((d, ns), lambda i, j: (0, 0)))
        out_specs = [out_specs, pl.BlockSpec((tm, ns), lambda i, j: (i, 0))]
        out_shape = [out_shape, jax.ShapeDtypeStruct((m, ns), F32)]
        args.append(side_w)
    return pl.pallas_call(
        functools.partial(_norm_matmul_kernel, relu2=relu2, side=side),
        grid=(m // tm, n // tn),
        in_specs=in_specs,
        out_specs=out_specs,
        out_shape=out_shape,
        scratch_shapes=[pltpu.VMEM((tm, d), BF16)],
        compiler_params=_params(("parallel", "arbitrary")),
        name=name,
    )(*args)


def _matmul_norm_res_kernel(*refs, n_in):
    a_refs = refs[:n_in]
    w_ref, g_ref, r_ref, o_ref, acc_ref = refs[n_in:]
    k = pl.program_id(1)

    @pl.when(k == 0)
    def _():
        acc_ref[...] = jnp.zeros_like(acc_ref)

    off = 0
    acc = acc_ref[...]
    for a_ref in a_refs:
        width = a_ref.shape[1]
        acc = acc + _dot(a_ref[...], w_ref[off:off + width, :])
        off += width
    acc_ref[...] = acc

    @pl.when(k == pl.num_programs(1) - 1)
    def _():
        o_ref[...] = r_ref[...] + _rms(acc_ref[...], g_ref[...])


def _matmul_norm_res(a_list, w_bf16, gain, resid, *, tm, tk, name):
    m, d = resid.shape
    ktot = w_bf16.shape[0]
    n_in = len(a_list)
    tm = min(tm, m)
    if n_in > 1:
        tk = ktot
        a_specs = [pl.BlockSpec((tm, a.shape[1]), lambda i, k: (i, 0)) for a in a_list]
    else:
        tk = min(tk, ktot)
        a_specs = [pl.BlockSpec((tm, tk), lambda i, k: (i, k))]
    return pl.pallas_call(
        functools.partial(_matmul_norm_res_kernel, n_in=n_in),
        grid=(m // tm, ktot // tk),
        in_specs=a_specs + [
            pl.BlockSpec((tk, d), lambda i, k: (k, 0)),
            pl.BlockSpec((1, d), lambda i, k: (0, 0)),
            pl.BlockSpec((tm, d), lambda i, k: (i, 0)),
        ],
        out_specs=pl.BlockSpec((tm, d), lambda i, k: (i, 0)),
        out_shape=jax.ShapeDtypeStruct((m, d), F32),
        scratch_shapes=[pltpu.VMEM((tm, d), F32)],
        compiler_params=_params(("parallel", "arbitrary")),
        name=name,
    )(*a_list, w_bf16, gain.reshape(1, d), resid)


def _gdn_kernel(q_ref, k_ref, v_ref, z_ref, ab_ref, cwq_ref, cwk_ref, cwv_ref, alog_ref, dtb_ref, nw_ref,
                o_ref, state_ref, tail_ref, pad_ref, qs_ref, ks_ref, vs_ref, gs_ref, bs_ref, os_ref,
                w2_ref, r_ref, qp_ref, op_ref, gl_ref):
    tt = q_ref.shape[0]
    c = GDN_CHUNK
    hd = HEAD_DIM
    hps = q_ref.shape[1] // hd
    n_chunks = tt // c
    head0 = pl.program_id(1) * hps
    lanes = lambda hh: slice(hh * hd, (hh + 1) * hd)

    @pl.when(pl.program_id(2) == 0)
    def _():
        state_ref[...] = jnp.zeros_like(state_ref)
        tail_ref[...] = jnp.zeros_like(tail_ref)

    def conv_silu(x_ref, slot, cw_ref):
        pad_ref[0:8, :] = tail_ref[slot]
        pad_ref[8:, :] = x_ref[...]
        tail_ref[slot] = x_ref[tt - 8:tt, :]
        y = cw_ref[0:1, :] * pad_ref[pl.ds(8 - (CONV_WIDTH - 1), tt), :]
        for j in range(1, CONV_WIDTH):
            y = y + cw_ref[j:j + 1, :] * pad_ref[pl.ds(8 - (CONV_WIDTH - 1) + j, tt), :]
        return y * jax.nn.sigmoid(y)

    def l2n(y):
        return y * lax.rsqrt(jnp.sum(y * y, axis=-1, keepdims=True) + NORM_EPS)

    qc = conv_silu(q_ref, 0, cwq_ref)
    kc = conv_silu(k_ref, 1, cwk_ref)
    vs_ref[...] = conv_silu(v_ref, 2, cwv_ref)
    ab = ab_ref[...]
    lane = lax.broadcasted_iota(jnp.int32, ab.shape, 1)
    for hh in range(hps):
        qs_ref[:, lanes(hh)] = l2n(qc[:, lanes(hh)]) * (HEAD_DIM ** -0.5)
        ks_ref[:, lanes(hh)] = l2n(kc[:, lanes(hh)])
        a_col = jnp.sum(jnp.where(lane == head0 + hh, ab, 0.0), axis=-1, keepdims=True)
        b_col = jnp.sum(jnp.where(lane == head0 + hh + GDN_HEADS, ab, 0.0), axis=-1, keepdims=True)
        xa = a_col + dtb_ref[hh]
        softplus = jnp.maximum(xa, 0.0) + jnp.log(1.0 + jnp.exp(-jnp.abs(xa)))
        gs_ref[:, lanes(hh)] = -(jnp.exp(alog_ref[hh]) * softplus)
        bs_ref[:, lanes(hh)] = jnp.broadcast_to(jax.nn.sigmoid(b_col), (tt, hd))

    row = lax.broadcasted_iota(jnp.int32, (c, c), 0)
    col = lax.broadcasted_iota(jnp.int32, (c, c), 1)
    causal = row >= col
    strict = row > col
    same_sub = (row // GDN_SUB) == (col // GDN_SUB)
    ltri3 = jnp.concatenate([causal.astype(BF16)] * 3, axis=1)

    units = [(n, hh) for n in range(n_chunks) for hh in range(hps)]
    each = lambda fn, *cols: [fn(*args) for args in zip(*cols)]
    rows = lambda ref: [ref[n * c:(n + 1) * c, lanes(hh)] for n, hh in units]
    q, k, v, g, beta = rows(qs_ref), rows(ks_ref), rows(vs_ref), rows(gs_ref), rows(bs_ref)
    pieces = each(_split3, g)
    gc = each(lambda p: _dot(ltri3, jnp.concatenate(p, axis=0)), pieces)
    dexp = each(lambda p: _dot(ltri3, jnp.concatenate([jnp.where(strict, x[:, :c], 0.0) for x in p], axis=0)), pieces)
    decay = each(lambda d: jnp.where(causal, jnp.exp(d), 0.0), dexp)
    kb = each(lambda a, b: a * b, k, beta)
    kbf = each(lambda a: a.astype(BF16), k)
    mm = each(lambda a, b, d: jnp.where(strict, _dot_nt(a.astype(BF16), b) * d, 0.0), kb, kbf, decay)
    a_qk = each(lambda a, b, d: (_dot_nt(a.astype(BF16), b) * d).astype(BF16), q, kbf, decay)
    x = each(lambda m_: jnp.where(same_sub, -m_, 0.0), mm)
    lo = each(lambda m_: jnp.where(same_sub, 0.0, m_), mm)
    x2 = each(_bdot, x, x)
    x4 = each(_bdot, x2, x2)
    a1 = each(lambda a, b: a + b + _bdot(a, b), x, x2)
    x8 = each(_bdot, x4, x4)
    a2 = each(lambda a, b: a + b + _bdot(a, b), a1, x4)
    dm = each(lambda a, b: a + b + _bdot(a, b), a2, x8)
    y = each(lambda d, l_: -(l_ + _bdot(d, l_)), dm, lo)
    y2 = each(_bdot, y, y)
    qm = each(lambda a, b: a + b + _bdot(a, b), y, y2)
    tm = each(lambda a, b: a + b + _bdot(a, b), qm, dm)
    eg = each(jnp.exp, gc)
    uw = each(lambda v_, b_, kb_, e_: jnp.concatenate([v_ * b_, kb_ * e_], axis=1), v, beta, kb, eg)
    uw = each(lambda t_, u_: (u_ + _bdot(t_, u_)).astype(BF16), tm, uw)
    au = each(_dot, a_qk, uw)
    kt = each(lambda k_, g_: (k_ * jnp.exp(g_[c - 1:c, :] - g_)).astype(BF16), k, gc)
    kw = each(_dot_tn, kt, uw)
    for u in range(len(units)):
        op_ref[u] = au[u][:, :hd]
        qp_ref[u] = (q[u] * eg[u] - au[u][:, hd:]).astype(BF16)
        r_ref[u] = kw[u][:, :hd]
        w2_ref[u] = kw[u][:, hd:].astype(BF16)
        gl_ref[u] = jnp.exp(gc[u][c - 1:c, :])

    s = [state_ref[hh] for hh in range(hps)]
    for u, (n, hh) in enumerate(units):
        sb = s[hh].astype(BF16)
        os_ref[n * c:(n + 1) * c, lanes(hh)] = op_ref[u] + _dot(qp_ref[u], sb)
        s[hh] = s[hh] * gl_ref[u] + r_ref[u] - _dot(w2_ref[u], sb)
    for hh in range(hps):
        state_ref[hh] = s[hh]
        z = z_ref[:, lanes(hh)]
        o_ref[:, lanes(hh)] = (_rms(os_ref[:, lanes(hh)], nw_ref[...]) * (z * jax.nn.sigmoid(z))).astype(o_ref.dtype)


def _gdn(proj_g, proj_ab, conv_w, a_log, dt_bias, norm_w, batch, seq, *, tt, hps):
    m = proj_g.shape[0]
    nt = seq // tt
    hd = HEAD_DIM
    wide = hps * hd
    ng = GDN_HEADS // hps
    n_units = (tt // GDN_CHUNK) * hps
    row = lambda b, hg, t: b * nt + t
    lane_b = lambda p: jnp.broadcast_to(p.astype(F32)[:, None, None], (GDN_HEADS, 1, hd))
    tile = lambda part: pl.BlockSpec((tt, wide), lambda b, hg, t: (row(b, hg, t), part * ng + hg))
    cw = lambda part: pl.BlockSpec((CONV_WIDTH, wide), lambda b, hg, t: (0, part * ng + hg))
    per_head = pl.BlockSpec((hps, 1, hd), lambda b, hg, t: (hg, 0, 0))
    return pl.pallas_call(
        _gdn_kernel,
        grid=(batch, ng, nt),
        in_specs=[
            tile(0), tile(1), tile(2), tile(3),
            pl.BlockSpec((tt, hd), lambda b, hg, t: (row(b, hg, t), 0)),
            cw(0), cw(1), cw(2),
            per_head, per_head,
            pl.BlockSpec((1, hd), lambda b, hg, t: (0, 0)),
        ],
        out_specs=pl.BlockSpec((tt, wide), lambda b, hg, t: (row(b, hg, t), hg)),
        out_shape=jax.ShapeDtypeStruct((m, GDN_HEADS * hd), BF16),
        scratch_shapes=[
            pltpu.VMEM((hps, hd, hd), F32),
            pltpu.VMEM((3, 8, wide), F32),
            pltpu.VMEM((tt + 8, wide), F32),
            pltpu.VMEM((tt, wide), F32),
            pltpu.VMEM((tt, wide), F32),
            pltpu.VMEM((tt, wide), F32),
            pltpu.VMEM((tt, wide), F32),
            pltpu.VMEM((tt, wide), F32),
            pltpu.VMEM((tt, wide), F32),
            pltpu.VMEM((n_units, hd, hd), BF16),
            pltpu.VMEM((n_units, hd, hd), F32),
            pltpu.VMEM((n_units, GDN_CHUNK, hd), BF16),
            pltpu.VMEM((n_units, GDN_CHUNK, hd), F32),
            pltpu.VMEM((n_units, 1, hd), F32),
        ],
        compiler_params=_params(("parallel", "parallel", "arbitrary")),
        name="gdn",
    )(proj_g, proj_g, proj_g, proj_g, proj_ab, conv_w, conv_w, conv_w,
      lane_b(a_log), lane_b(dt_bias), norm_w.reshape(1, hd))


def _moba_kernel(q_ref, k_ref, v_ref, slope_ref, o_ref, kaug_ref, qaug_ref, kmean_ref, sel_ref, sa_ref, sb_ref, acc_ref):
    blk = MOBA_BLOCK
    hd = HEAD_DIM
    grp = MOBA_GROUP
    hps = q_ref.shape[1] // hd
    heads = range(hps)
    nb = k_ref.shape[0] // blk
    i = pl.program_id(2)
    lanes = lambda hh: slice(hh * hd, (hh + 1) * hd)
    slope2 = [slope_ref[hh] for hh in heads]

    @pl.when(i == 0)
    def _():
        pos = lax.broadcasted_iota(jnp.int32, (blk, hd), 0).astype(F32)
        lane = lax.broadcasted_iota(jnp.int32, (blk, hd), 1)
        one = jnp.ones((blk, hd), BF16)
        zero = jnp.zeros((blk, hd), BF16)
        k_extra = []
        for hh in heads:
            hi, mid, lo = _split3(slope2[hh][:, :hd] * pos)
            pieces = jnp.where(lane == 0, hi, jnp.where(lane == 1, mid, jnp.where(lane == 2, lo, zero)))
            k_extra.append(jnp.where((lane >= 3) & (lane < 6), one, pieces))
            npieces = jnp.where(lane == 3, -hi, jnp.where(lane == 4, -mid, jnp.where(lane == 5, -lo, zero)))
            qaug_ref[hh, :, hd:] = jnp.where(lane < 3, one, npieces)

        def block_step(j, carry):
            r0 = pl.multiple_of(j * blk, blk)
            for hh in heads:
                kb = k_ref[pl.ds(r0, blk), lanes(hh)]
                kaug_ref[hh, pl.ds(r0, blk), 0:hd] = kb
                kaug_ref[hh, pl.ds(r0, blk), hd:] = k_extra[hh]
                kmean_ref[hh, pl.ds(j, 1), :] = jnp.mean(kb.astype(F32), axis=0, keepdims=True)
            return carry
        lax.fori_loop(0, nb, block_step, 0)

    blk_id = lax.broadcasted_iota(jnp.int32, (nb, blk), 0)
    key_pos = lax.broadcasted_iota(jnp.int32, (blk, blk), 0)
    qry_pos = lax.broadcasted_iota(jnp.int32, (blk, blk), 1)
    r0 = pl.multiple_of(i * blk, blk)
    def score_group(g, dst_ref):
        for b in range(grp):
            c0 = pl.multiple_of((g * grp + b) * blk, blk)
            for hh in heads:
                dst_ref[hh * grp + b] = _dot_nt(kaug_ref[hh, pl.ds(c0, blk), :], qa[hh])

    qa, gates, own = [], [], []
    for hh in heads:
        q = q_ref[:, lanes(hh)]
        qaug_ref[hh, :, 0:hd] = q
        qa.append(qaug_ref[hh])
        gates.append(lax.dot_general(kmean_ref[hh], q.astype(F32), (((1,), (1,)), ((), ())),
                                     precision=HIGHEST, preferred_element_type=F32))
        own.append(_dot_nt(kaug_ref[hh, pl.ds(r0, blk), :], qa[hh]))
    score_group(0, sa_ref)

    start = []
    for hh in heads:
        gate = jnp.where(blk_id < i, gates[hh], NEG_INF)
        sel = jnp.zeros((nb, blk), F32)
        for _ in range(MOBA_TOPK):
            best = jnp.max(gate, axis=0, keepdims=True)
            hit = (gate == best) & (best > NEG_INF)
            first = jnp.min(jnp.where(hit, blk_id, nb), axis=0, keepdims=True)
            pick = blk_id == first
            sel = jnp.where(pick, 1.0, sel)
            gate = jnp.where(pick, NEG_INF, gate)
        sel_ref[hh] = sel
        s = jnp.where(key_pos <= qry_pos, own[hh], NEG_INF)
        m0 = jnp.max(s, axis=0, keepdims=True)
        p = jnp.exp2(s - m0)
        l0 = jnp.sum(p, axis=0, keepdims=True)
        acc_ref[hh] = _dot_tn(v_ref[pl.ds(r0, blk), lanes(hh)], p.astype(BF16))
        start.append((m0, l0))

    def group_max(g, src_ref, carry):
        subs = [[] for _ in heads]
        m_new = [carry[hh][0] for hh in heads]
        for b in range(grp):
            j = g * grp + b
            for hh in heads:
                chosen = sel_ref[hh, pl.ds(j, 1), :] > 0.0
                off = slope2[hh] * ((i - j) * blk).astype(F32)
                top = jnp.max(src_ref[hh * grp + b], axis=0, keepdims=True) - off
                m_new[hh] = jnp.maximum(m_new[hh], jnp.where(chosen, top, NEG_INF))
                subs[hh].append((chosen, off))
        return m_new, subs

    def attend_group(g, src_ref, carry, stats):
        m_new, subs = stats
        alpha = [jnp.exp2(carry[hh][0] - m_new[hh]) for hh in heads]
        l_new = [alpha[hh] * carry[hh][1] for hh in heads]
        pv = [None for _ in heads]
        for b in range(grp):
            c0 = pl.multiple_of((g * grp + b) * blk, blk)
            for hh in heads:
                chosen, off = subs[hh][b]
                p2 = jnp.exp2(src_ref[hh * grp + b] - jnp.where(chosen, m_new[hh] + off, jnp.inf))
                l_new[hh] = l_new[hh] + jnp.sum(p2, axis=0, keepdims=True)
                part = _dot_tn(v_ref[pl.ds(c0, blk), lanes(hh)], p2.astype(BF16))
                pv[hh] = part if pv[hh] is None else pv[hh] + part
        for hh in heads:
            acc_ref[hh] = acc_ref[hh] * alpha[hh] + pv[hh]
        return tuple((m_new[hh], l_new[hh]) for hh in heads)

    last_group = nb // grp - 1
    n_pairs = (i + 2 * grp - 1) // (2 * grp)

    def past_pair(gp, carry):
        g = 2 * gp
        stats = group_max(g, sa_ref, carry)
        score_group(g + 1, sb_ref)
        carry = attend_group(g, sa_ref, carry, stats)
        stats = group_max(g + 1, sb_ref, carry)
        score_group(jnp.minimum(g + 2, last_group), sa_ref)
        return attend_group(g + 1, sb_ref, carry, stats)

    final = lax.fori_loop(0, n_pairs, past_pair, tuple(start))
    for hh in heads:
        _, l = final[hh]
        o_ref[:, lanes(hh)] = (acc_ref[hh] / l).T.astype(o_ref.dtype)


def _moba(proj_m, slopes, batch, seq, *, hps):
    m = proj_m.shape[0]
    blk, hd = MOBA_BLOCK, HEAD_DIM
    wide = hps * hd
    ng = MOBA_HEADS // hps
    nb = seq // blk
    assert nb % (2 * MOBA_GROUP) == 0
    slope_b = jnp.broadcast_to((slopes.astype(F32) * LOG2E)[:, None, None], (MOBA_HEADS, 1, blk))
    whole_seq = lambda part: pl.BlockSpec((seq, wide), lambda b, hg, i: (b, part * ng + hg),
                                          pipeline_mode=pl.Buffered(1))
    return pl.pallas_call(
        _moba_kernel,
        grid=(batch, ng, nb),
        in_specs=[
            pl.BlockSpec((blk, wide), lambda b, hg, i: (b * nb + i, hg)),
            whole_seq(1),
            whole_seq(2),
            pl.BlockSpec((hps, 1, blk), lambda b, hg, i: (hg, 0, 0)),
        ],
        out_specs=pl.BlockSpec((blk, wide), lambda b, hg, i: (b * nb + i, hg)),
        out_shape=jax.ShapeDtypeStruct((m, MOBA_HEADS * hd), BF16),
        scratch_shapes=[
            pltpu.VMEM((hps, seq, 2 * hd), BF16),
            pltpu.VMEM((hps, blk, 2 * hd), BF16),
            pltpu.VMEM((hps, nb, hd), F32),
            pltpu.VMEM((hps, nb, blk), F32),
            pltpu.VMEM((hps * MOBA_GROUP, blk, blk), F32),
            pltpu.VMEM((hps * MOBA_GROUP, blk, blk), F32),
            pltpu.VMEM((hps, hd, blk), F32),
        ],
        compiler_params=_params(("parallel", "parallel", "arbitrary")),
        name="moba",
    )(proj_m, proj_m, proj_m, slope_b)


def _mem_attn_kernel(x_ref, gpre_ref, wq_ref, kv_ref, wo_ref, gpost_ref, o_ref):
    hd = HEAD_DIM
    width = MEM_HEADS * hd
    x = x_ref[...]
    hn = _rms(x, gpre_ref[...]).astype(BF16)
    q = (_dot(hn, wq_ref[...]) * (hd ** -0.5)).astype(BF16)
    kv = kv_ref[...]
    heads = []
    for hh in range(MEM_HEADS):
        s = _dot_nt(q[:, hh * hd:(hh + 1) * hd], kv[:, hh * hd:(hh + 1) * hd])
        p = jnp.exp(s - jnp.max(s, axis=-1, keepdims=True))
        p = p / jnp.sum(p, axis=-1, keepdims=True)
        heads.append(_dot(p.astype(BF16), kv[:, width + hh * hd:width + (hh + 1) * hd]))
    o = jnp.concatenate(heads, axis=-1).astype(BF16)
    o_ref[...] = x + _rms(_dot(o, wo_ref[...]), gpost_ref[...])


def _mem_attn(x2d, g_pre, wq_bf16, kv_bf16, wo_bf16, g_post, batch, seq, *, tm):
    m, d = x2d.shape
    nt = seq // tm
    mem_len, kvw = kv_bf16.shape[1], kv_bf16.shape[2]
    width = wq_bf16.shape[1]
    return pl.pallas_call(
        _mem_attn_kernel,
        grid=(batch, nt),
        in_specs=[
            pl.BlockSpec((tm, d), lambda b, t: (b * nt + t, 0)),
            pl.BlockSpec((1, d), lambda b, t: (0, 0)),
            pl.BlockSpec((d, width), lambda b, t: (0, 0)),
            pl.BlockSpec((None, mem_len, kvw), lambda b, t: (b, 0, 0)),
            pl.BlockSpec((width, d), lambda b, t: (0, 0)),
            pl.BlockSpec((1, d), lambda b, t: (0, 0)),
        ],
        out_specs=pl.BlockSpec((tm, d), lambda b, t: (b * nt + t, 0)),
        out_shape=jax.ShapeDtypeStruct((m, d), F32),
        compiler_params=_params(("parallel", "parallel")),
        name="mem_attn",
    )(x2d, g_pre.reshape(1, d), wq_bf16, kv_bf16, wo_bf16, g_post.reshape(1, d))


def kernel(x, mem, pre_mix_norm, w_in, conv_w, a_log, dt_bias, gdn_norm_w, w_out, post_mix_norm,
           pre_mem_norm, mem_kv_norm, w_mq, w_mk, w_mv, w_mo, post_mem_norm,
           pre_mlp_norm, w_up, w_down, post_mlp_norm):
    batch, seq, d = x.shape
    mem_len = mem.shape[1]
    gw = GDN_HEADS * HEAD_DIM
    mw = MOBA_HEADS * HEAD_DIM
    assert seq % MOBA_BLOCK == 0 and seq % 512 == 0
    x2d = x.reshape(batch * seq, d)
    mem2d = mem.reshape(batch * mem_len, d)
    slopes = jnp.exp2(-8.0 * jnp.arange(1, MOBA_HEADS + 1, dtype=F32) / MOBA_HEADS)
    ones = lambda n: jnp.ones((n,), F32)

    for l in range(w_in.shape[0]):
        win = w_in[l]
        w_g = win[:, :4 * gw].astype(BF16)
        w_ab = jnp.pad(win[:, 4 * gw:4 * gw + 2 * GDN_HEADS], ((0, 0), (0, HEAD_DIM - 2 * GDN_HEADS))).astype(BF16)
        w_m = win[:, 4 * gw + 2 * GDN_HEADS:].astype(BF16)
        q_scale = jnp.concatenate([jnp.full((mw,), LOG2E * HEAD_DIM ** -0.5, F32), ones(2 * mw)])

        proj_g, proj_ab = _norm_matmul(x2d, pre_mix_norm[l], w_g, ones(4 * gw), F32, tm=1024, tn=1024,
                                       side_w=w_ab, name="in_proj_gdn")
        proj_m = _norm_matmul(x2d, pre_mix_norm[l], w_m, q_scale, BF16, tm=1024, tn=1024, name="in_proj_moba")

        y_gdn = _gdn(proj_g, proj_ab, conv_w[l], a_log[l], dt_bias[l], gdn_norm_w[l], batch, seq, tt=256, hps=4)
        y_moba = _moba(proj_m, slopes, batch, seq, hps=2)
        x2d = _matmul_norm_res([y_gdn, y_moba], w_out[l].astype(BF16), post_mix_norm[l], x2d,
                               tm=512, tk=gw + mw, name="out_proj")

        w_kv = jnp.concatenate([w_mk[l], w_mv[l]], axis=1).astype(BF16)
        kv = _norm_matmul(mem2d, mem_kv_norm[l], w_kv, ones(w_kv.shape[1]), BF16, tm=512, tn=1024, name="mem_kv")
        x2d = _mem_attn(x2d, pre_mem_norm[l], w_mq[l].astype(BF16), kv.reshape(batch, mem_len, -1),
                        w_mo[l].astype(BF16), post_mem_norm[l], batch, seq, tm=512)

        hid = _norm_matmul(x2d, pre_mlp_norm[l], w_up[l].astype(BF16), ones(w_up.shape[2]), BF16,
                           tm=1024, tn=1024, relu2=True, name="mlp_up")
        x2d = _matmul_norm_res([hid], w_down[l].astype(BF16), post_mlp_norm[l], x2d, tm=512, tk=2048, name="mlp_down")
    return x2d.reshape(batch, seq, d)
```

```python
import functools
import math

import jax
import jax.numpy as jnp
from jax import lax
from jax.experimental import pallas as pl
from jax.experimental.pallas import tpu as pltpu

F32 = jnp.float32
BF16 = jnp.bfloat16
HIGHEST = lax.Precision.HIGHEST

HEAD_DIM = 128
GDN_HEADS = 8
MOBA_HEADS = 8
CONV_WIDTH = 4
GDN_CHUNK = 64
GDN_SUB = 16
MOBA_BLOCK = 256
MOBA_TOPK = 3
MOBA_GROUP = 4
MOBA_VROWS = HEAD_DIM + 16
MEM_HEADS = 4
NORM_EPS = 1e-6
LOG2E = math.log2(math.e)

V7X_VMEM_BYTES = 64 * 1024 * 1024
VMEM_LIMIT = V7X_VMEM_BYTES - 8 * 1024 * 1024

NEG_INF = float("-inf")


def _params(semantics):
    return pltpu.CompilerParams(dimension_semantics=semantics, vmem_limit_bytes=VMEM_LIMIT)


def _dot(a, b):
    return jnp.dot(a, b, preferred_element_type=F32)


def _dot_nt(a, b):
    return lax.dot_general(a, b, (((1,), (1,)), ((), ())), preferred_element_type=F32)


def _dot_tn(a, b):
    return lax.dot_general(a, b, (((0,), (0,)), ((), ())), preferred_element_type=F32)


def _bdot(a, b):
    return _dot(a.astype(BF16), b.astype(BF16))


def _split3(x):
    hi = x.astype(BF16)
    rest = x - hi.astype(F32)
    mid = rest.astype(BF16)
    lo = (rest - mid.astype(F32)).astype(BF16)
    return hi, mid, lo


def _rms(y, gain):
    return y * lax.rsqrt(jnp.mean(y * y, axis=-1, keepdims=True) + NORM_EPS) * gain


def _norm_matmul_kernel(*refs, relu2, side):
    if side:
        x_ref, g_ref, w_ref, cs_ref, ws_ref, o_ref, os_ref, hn_ref = refs
    else:
        x_ref, g_ref, w_ref, cs_ref, o_ref, hn_ref = refs

    @pl.when(pl.program_id(1) == 0)
    def _():
        hn_ref[...] = _rms(x_ref[...], g_ref[...]).astype(BF16)
        if side:
            os_ref[...] = _dot(hn_ref[...], ws_ref[...])

    y = _dot(hn_ref[...], w_ref[...]) * cs_ref[...]
    if relu2:
        y = jnp.square(jnp.maximum(y, 0.0))
    o_ref[...] = y.astype(o_ref.dtype)


def _norm_matmul(x2d, gain, w_bf16, col_scale, out_dtype, *, tm, tn, relu2=False, side_w=None, name):
    m, d = x2d.shape
    n = w_bf16.shape[1]
    tm, tn = min(tm, m), min(tn, n)
    side = side_w is not None
    in_specs = [
        pl.BlockSpec((tm, d), lambda i, j: (i, 0)),
        pl.BlockSpec((1, d), lambda i, j: (0, 0)),
        pl.BlockSpec((d, tn), lambda i, j: (0, j)),
        pl.BlockSpec((1, tn), lambda i, j: (0, j)),
    ]
    out_specs = pl.BlockSpec((tm, tn), lambda i, j: (i, j))
    out_shape = jax.ShapeDtypeStruct((m, n), out_dtype)
    args = [x2d, gain.reshape(1, d), w_bf16, col_scale.reshape(1, n)]
    if side:
        ns = side_w.shape[1]
        in_specs.append(pl.BlockSpec((d, ns), lambda i, j: (0, 0)))
        out_specs = [out_specs, pl.BlockSpec((tm, ns), lambda i, j: (i, 0))]
        out_shape = [out_shape, jax.ShapeDtypeStruct((m, ns), F32)]
        args.append(side_w)
    return pl.pallas_call(
        functools.partial(_norm_matmul_kernel, relu2=relu2, side=side),
        grid=(m // tm, n // tn),
        in_specs=in_specs,
        out_specs=out_specs,
        out_shape=out_shape,
        scratch_shapes=[pltpu.VMEM((tm, d), BF16)],
        compiler_params=_params(("parallel", "arbitrary")),
        name=name,
    )(*args)


def _matmul_norm_res_kernel(*refs, n_in):
    a_refs = refs[:n_in]
    w_ref, g_ref, r_ref, o_ref, acc_ref = refs[n_in:]
    k = pl.program_id(1)

    @pl.when(k == 0)
    def _():
        acc_ref[...] = jnp.zeros_like(acc_ref)

    off = 0
    acc = acc_ref[...]
    for a_ref in a_refs:
        width = a_ref.shape[1]
        acc = acc + _dot(a_ref[...], w_ref[off:off + width, :])
        off += width
    acc_ref[...] = acc

    @pl.when(k == pl.num_programs(1) - 1)
    def _():
        o_ref[...] = r_ref[...] + _rms(acc_ref[...], g_ref[...])


def _matmul_norm_res(a_list, w_bf16, gain, resid, *, tm, tk, name):
    m, d = resid.shape
    ktot = w_bf16.shape[0]
    n_in = len(a_list)
    tm = min(tm, m)
    if n_in > 1:
        tk = ktot
        a_specs = [pl.BlockSpec((tm, a.shape[1]), lambda i, k: (i, 0)) for a in a_list]
    else:
        tk = min(tk, ktot)
        a_specs = [pl.BlockSpec((tm, tk), lambda i, k: (i, k))]
    return pl.pallas_call(
        functools.partial(_matmul_norm_res_kernel, n_in=n_in),
        grid=(m // tm, ktot // tk),
        in_specs=a_specs + [
            pl.BlockSpec((tk, d), lambda i, k: (k, 0)),
            pl.BlockSpec((1, d), lambda i, k: (0, 0)),
            pl.BlockSpec((tm, d), lambda i, k: (i, 0)),
        ],
        out_specs=pl.BlockSpec((tm, d), lambda i, k: (i, 0)),
        out_shape=jax.ShapeDtypeStruct((m, d), F32),
        scratch_shapes=[pltpu.VMEM((tm, d), F32)],
        compiler_params=_params(("parallel", "arbitrary")),
        name=name,
    )(*a_list, w_bf16, gain.reshape(1, d), resid)


def _gdn_kernel(q_ref, k_ref, v_ref, z_ref, ab_ref, cwq_ref, cwk_ref, cwv_ref, alog_ref, dtb_ref, nw_ref,
                o_ref, state_ref, tail_ref, pad_ref, qs_ref, ks_ref, vs_ref, gs_ref, bs_ref, os_ref,
                w2_ref, r_ref, qp_ref, op_ref, gl_ref):
    tt = q_ref.shape[0]
    c = GDN_CHUNK
    hd = HEAD_DIM
    hps = q_ref.shape[1] // hd
    n_chunks = tt // c
    head0 = pl.program_id(1) * hps
    lanes = lambda hh: slice(hh * hd, (hh + 1) * hd)

    @pl.when(pl.program_id(2) == 0)
    def _():
        state_ref[...] = jnp.zeros_like(state_ref)
        tail_ref[...] = jnp.zeros_like(tail_ref)

    def conv_silu(x_ref, slot, cw_ref):
        pad_ref[0:8, :] = tail_ref[slot]
        pad_ref[8:, :] = x_ref[...]
        tail_ref[slot] = x_ref[tt - 8:tt, :]
        y = cw_ref[0:1, :] * pad_ref[pl.ds(8 - (CONV_WIDTH - 1), tt), :]
        for j in range(1, CONV_WIDTH):
            y = y + cw_ref[j:j + 1, :] * pad_ref[pl.ds(8 - (CONV_WIDTH - 1) + j, tt), :]
        return y * jax.nn.sigmoid(y)

    def l2n(y):
        return y * lax.rsqrt(jnp.sum(y * y, axis=-1, keepdims=True) + NORM_EPS)

    qc = conv_silu(q_ref, 0, cwq_ref)
    kc = conv_silu(k_ref, 1, cwk_ref)
    vs_ref[...] = conv_silu(v_ref, 2, cwv_ref)
    ab = ab_ref[...]
    lane = lax.broadcasted_iota(jnp.int32, ab.shape, 1)
    xa = ab + dtb_ref[...]
    softplus = jnp.maximum(xa, 0.0) + jnp.log(1.0 + jnp.exp(-jnp.abs(xa)))
    g_all = -(jnp.exp(alog_ref[...]) * softplus)
    beta_all = jax.nn.sigmoid(ab)
    for hh in range(hps):
        qs_ref[:, lanes(hh)] = l2n(qc[:, lanes(hh)]) * (HEAD_DIM ** -0.5)
        ks_ref[:, lanes(hh)] = l2n(kc[:, lanes(hh)])
        g_col = jnp.sum(jnp.where(lane == head0 + hh, g_all, 0.0), axis=-1, keepdims=True)
        b_col = jnp.sum(jnp.where(lane == head0 + hh + GDN_HEADS, beta_all, 0.0), axis=-1, keepdims=True)
        gs_ref[:, lanes(hh)] = jnp.broadcast_to(g_col, (tt, hd))
        bs_ref[:, lanes(hh)] = jnp.broadcast_to(b_col, (tt, hd))

    row = lax.broadcasted_iota(jnp.int32, (c, c), 0)
    col = lax.broadcasted_iota(jnp.int32, (c, c), 1)
    causal = row >= col
    strict = row > col
    same_sub = (row // GDN_SUB) == (col // GDN_SUB)
    ltri3 = jnp.concatenate([causal.astype(BF16)] * 3, axis=1)

    units = [(n, hh) for n in range(n_chunks) for hh in range(hps)]
    each = lambda fn, *cols: [fn(*args) for args in zip(*cols)]
    rows = lambda ref: [ref[n * c:(n + 1) * c, lanes(hh)] for n, hh in units]
    q, k, v, g, beta = rows(qs_ref), rows(ks_ref), rows(vs_ref), rows(gs_ref), rows(bs_ref)
    pieces = each(_split3, g)
    gc = each(lambda p: _dot(ltri3, jnp.concatenate(p, axis=0)), pieces)
    dexp = each(lambda p: _dot(ltri3, jnp.concatenate([jnp.where(strict, x[:, :c], 0.0) for x in p], axis=0)), pieces)
    decay = each(lambda d: jnp.where(causal, jnp.exp(d), 0.0), dexp)
    kb = each(lambda a, b: a * b, k, beta)
    kbf = each(lambda a: a.astype(BF16), k)
    mm = each(lambda a, b, d: jnp.where(strict, _dot_nt(a.astype(BF16), b) * d, 0.0), kb, kbf, decay)
    a_qk = each(lambda a, b, d: (_dot_nt(a.astype(BF16), b) * d).astype(BF16), q, kbf, decay)
    x = each(lambda m_: jnp.where(same_sub, -m_, 0.0), mm)
    lo = each(lambda m_: jnp.where(same_sub, 0.0, m_), mm)
    x2 = each(_bdot, x, x)
    x4 = each(_bdot, x2, x2)
    a1 = each(lambda a, b: a + b + _bdot(a, b), x, x2)
    x8 = each(_bdot, x4, x4)
    a2 = each(lambda a, b: a + b + _bdot(a, b), a1, x4)
    dm = each(lambda a, b: a + b + _bdot(a, b), a2, x8)
    y = each(lambda d, l_: -(l_ + _bdot(d, l_)), dm, lo)
    y2 = each(_bdot, y, y)
    qm = each(lambda a, b: a + b + _bdot(a, b), y, y2)
    tm = each(lambda a, b: a + b + _bdot(a, b), qm, dm)
    eg = each(jnp.exp, gc)
    uw = each(lambda v_, b_, kb_, e_: jnp.concatenate([v_ * b_, kb_ * e_], axis=1), v, beta, kb, eg)
    uw = each(lambda t_, u_: (u_ + _bdot(t_, u_)).astype(BF16), tm, uw)
    au = each(_dot, a_qk, uw)
    kt = each(lambda k_, g_: (k_ * jnp.exp(g_[c - 1:c, :] - g_)).astype(BF16), k, gc)
    kw = each(_dot_tn, kt, uw)
    for u in range(len(units)):
        op_ref[u] = au[u][:, :hd]
        qp_ref[u] = (q[u] * eg[u] - au[u][:, hd:]).astype(BF16)
        r_ref[u] = kw[u][:, :hd]
        w2_ref[u] = kw[u][:, hd:].astype(BF16)
        gl_ref[u] = jnp.exp(gc[u][c - 1:c, :])

    s = [state_ref[hh] for hh in range(hps)]
    for u, (n, hh) in enumerate(units):
        sb = s[hh].astype(BF16)
        os_ref[n * c:(n + 1) * c, lanes(hh)] = op_ref[u] + _dot(qp_ref[u], sb)
        s[hh] = s[hh] * gl_ref[u] + r_ref[u] - _dot(w2_ref[u], sb)
    for hh in range(hps):
        state_ref[hh] = s[hh]
        z = z_ref[:, lanes(hh)]
        o_ref[:, lanes(hh)] = (_rms(os_ref[:, lanes(hh)], nw_ref[...]) * (z * jax.nn.sigmoid(z))).astype(o_ref.dtype)


def _gdn(proj_g, proj_ab, conv_w, a_log, dt_bias, norm_w, batch, seq, *, tt, hps):
    m = proj_g.shape[0]
    nt = seq // tt
    hd = HEAD_DIM
    wide = hps * hd
    ng = GDN_HEADS // hps
    n_units = (tt // GDN_CHUNK) * hps
    row = lambda b, hg, t: b * nt + t
    on_a_lanes = lambda p: jnp.pad(p.astype(F32), (0, hd - GDN_HEADS)).reshape(1, hd)
    tile = lambda part: pl.BlockSpec((tt, wide), lambda b, hg, t: (row(b, hg, t), part * ng + hg))
    cw = lambda part: pl.BlockSpec((CONV_WIDTH, wide), lambda b, hg, t: (0, part * ng + hg))
    per_head = pl.BlockSpec((1, hd), lambda b, hg, t: (0, 0))
    return pl.pallas_call(
        _gdn_kernel,
        grid=(batch, ng, nt),
        in_specs=[
            tile(0), tile(1), tile(2), tile(3),
            pl.BlockSpec((tt, hd), lambda b, hg, t: (row(b, hg, t), 0)),
            cw(0), cw(1), cw(2),
            per_head, per_head,
            pl.BlockSpec((1, hd), lambda b, hg, t: (0, 0)),
        ],
        out_specs=pl.BlockSpec((tt, wide), lambda b, hg, t: (row(b, hg, t), hg)),
        out_shape=jax.ShapeDtypeStruct((m, GDN_HEADS * hd), BF16),
        scratch_shapes=[
            pltpu.VMEM((hps, hd, hd), F32),
            pltpu.VMEM((3, 8, wide), F32),
            pltpu.VMEM((tt + 8, wide), F32),
            pltpu.VMEM((tt, wide), F32),
            pltpu.VMEM((tt, wide), F32),
            pltpu.VMEM((tt, wide), F32),
            pltpu.VMEM((tt, wide), F32),
            pltpu.VMEM((tt, wide), F32),
            pltpu.VMEM((tt, wide), F32),
            pltpu.VMEM((n_units, hd, hd), BF16),
            pltpu.VMEM((n_units, hd, hd), F32),
            pltpu.VMEM((n_units, GDN_CHUNK, hd), BF16),
            pltpu.VMEM((n_units, GDN_CHUNK, hd), F32),
            pltpu.VMEM((n_units, 1, hd), F32),
        ],
        compiler_params=_params(("parallel", "parallel", "arbitrary")),
        name="gdn",
    )(proj_g, proj_g, proj_g, proj_g, proj_ab, conv_w, conv_w, conv_w,
      on_a_lanes(a_log), on_a_lanes(dt_bias), norm_w.reshape(1, hd))


def _moba_kernel(q_ref, k_ref, v_ref, slope_ref, o_ref, kaug_ref, qaug_ref, kmean_ref, sel_ref, sa_ref, sb_ref, acc_ref, vt_ref,
                 ta_ref, tb_ref):
    blk = MOBA_BLOCK
    hd = HEAD_DIM
    grp = MOBA_GROUP
    hps = q_ref.shape[1] // hd
    heads = range(hps)
    nb = k_ref.shape[0] // blk
    i = pl.program_id(2)
    lanes = lambda hh: slice(hh * hd, (hh + 1) * hd)
    slope2 = [slope_ref[hh] for hh in heads]

    @pl.when(i == 0)
    def _():
        pos = lax.broadcasted_iota(jnp.int32, (blk, hd), 0).astype(F32)
        lane = lax.broadcasted_iota(jnp.int32, (blk, hd), 1)
        one = jnp.ones((blk, hd), BF16)
        zero = jnp.zeros((blk, hd), BF16)
        ones_row = (lax.broadcasted_iota(jnp.int32, (MOBA_VROWS - hd, blk), 0) == 0).astype(BF16)
        k_extra = []
        for hh in heads:
            hi, mid, lo = _split3(slope2[hh][:, :hd] * pos)
            pieces = jnp.where(lane == 0, hi, jnp.where(lane == 1, mid, jnp.where(lane == 2, lo, zero)))
            k_extra.append(jnp.where((lane >= 3) & (lane < 6), one, pieces))
            npieces = jnp.where(lane == 3, -hi, jnp.where(lane == 4, -mid, jnp.where(lane == 5, -lo, zero)))
            qaug_ref[hh, :, hd:] = jnp.where(lane < 3, one, npieces)

        def block_step(j, carry):
            r0 = pl.multiple_of(j * blk, blk)
            for hh in heads:
                kb = k_ref[pl.ds(r0, blk), lanes(hh)]
                kaug_ref[hh, pl.ds(r0, blk), 0:hd] = kb
                kaug_ref[hh, pl.ds(r0, blk), hd:] = k_extra[hh]
                kmean_ref[hh, pl.ds(j, 1), :] = jnp.mean(kb.astype(F32), axis=0, keepdims=True)
                vt_ref[hh, j, 0:hd, :] = v_ref[pl.ds(r0, blk), lanes(hh)].astype(F32).T.astype(BF16)
                vt_ref[hh, j, hd:, :] = ones_row
            return carry
        lax.fori_loop(0, nb, block_step, 0)

    blk_id = lax.broadcasted_iota(jnp.int32, (nb, blk), 0)
    key_pos = lax.broadcasted_iota(jnp.int32, (blk, blk), 0)
    qry_pos = lax.broadcasted_iota(jnp.int32, (blk, blk), 1)
    r0 = pl.multiple_of(i * blk, blk)
    def score_group(g, dst_ref, top_ref):
        for b in range(grp):
            c0 = pl.multiple_of((g * grp + b) * blk, blk)
            for hh in heads:
                s2 = _dot_nt(kaug_ref[hh, pl.ds(c0, blk), :], qa[hh])
                dst_ref[hh * grp + b] = s2
                top_ref[hh * grp + b] = jnp.max(s2, axis=0, keepdims=True)

    qa, gates, own = [], [], []
    for hh in heads:
        q = q_ref[:, lanes(hh)]
        qaug_ref[hh, :, 0:hd] = q
        qa.append(qaug_ref[hh])
        gates.append(lax.dot_general(kmean_ref[hh], q.astype(F32), (((1,), (1,)), ((), ())),
                                     precision=HIGHEST, preferred_element_type=F32))
        own.append(_dot_nt(kaug_ref[hh, pl.ds(r0, blk), :], qa[hh]))
    score_group(0, sa_ref, ta_ref)

    start = []
    for hh in heads:
        gate = jnp.where(blk_id < i, gates[hh], NEG_INF)
        sel = jnp.zeros((nb, blk), F32)
        for _ in range(MOBA_TOPK):
            best = jnp.max(gate, axis=0, keepdims=True)
            hit = (gate == best) & (best > NEG_INF)
            first = jnp.min(jnp.where(hit, blk_id, nb), axis=0, keepdims=True)
            pick = blk_id == first
            sel = jnp.where(pick, 1.0, sel)
            gate = jnp.where(pick, NEG_INF, gate)
        sel_ref[hh] = sel
        s = jnp.where(key_pos <= qry_pos, own[hh], NEG_INF)
        m0 = jnp.max(s, axis=0, keepdims=True)
        p = jnp.exp2(s - m0)
        acc_ref[hh] = _dot(vt_ref[hh, i], p.astype(BF16))
        start.append(m0)

    def group_max(g, top_ref, carry):
        subs = [[] for _ in heads]
        m_new = [carry[hh] for hh in heads]
        for b in range(grp):
            j = g * grp + b
            for hh in heads:
                chosen = sel_ref[hh, pl.ds(j, 1), :] > 0.0
                off = slope2[hh] * ((i - j) * blk).astype(F32)
                top = top_ref[hh * grp + b] - off
                m_new[hh] = jnp.maximum(m_new[hh], jnp.where(chosen, top, NEG_INF))
                subs[hh].append((chosen, off))
        return m_new, subs

    def attend_group(g, src_ref, carry, stats):
        m_new, subs = stats
        alpha = [jnp.exp2(carry[hh] - m_new[hh]) for hh in heads]
        pv = [None for _ in heads]
        for b in range(grp):
            for hh in heads:
                chosen, off = subs[hh][b]
                p2 = jnp.exp2((src_ref[hh * grp + b] - jnp.where(chosen, m_new[hh] + off, jnp.inf)).astype(BF16))
                part = _dot(vt_ref[hh, g * grp + b], p2)
                pv[hh] = part if pv[hh] is None else pv[hh] + part
        for hh in heads:
            acc_ref[hh] = acc_ref[hh] * alpha[hh] + pv[hh]
        return tuple(m_new)

    last_group = nb // grp - 1
    n_groups = (i + grp - 1) // grp
    n_pairs = n_groups // 2

    def past_pair(gp, carry):
        g = 2 * gp
        stats = group_max(g, ta_ref, carry)
        score_group(g + 1, sb_ref, tb_ref)
        carry = attend_group(g, sa_ref, carry, stats)
        stats = group_max(g + 1, tb_ref, carry)
        score_group(jnp.minimum(g + 2, last_group), sa_ref, ta_ref)
        return attend_group(g + 1, sb_ref, carry, stats)

    final = lax.fori_loop(0, n_pairs, past_pair, tuple(start))

    @pl.when(n_groups % 2 == 1)
    def _():
        attend_group(n_groups - 1, sa_ref, final, group_max(n_groups - 1, ta_ref, final))

    for hh in heads:
        o_ref[:, lanes(hh)] = (acc_ref[hh, 0:hd, :] / acc_ref[hh, hd:hd + 1, :]).T.astype(o_ref.dtype)


def _moba(proj_m, slopes, batch, seq, *, hps):
    m = proj_m.shape[0]
    blk, hd = MOBA_BLOCK, HEAD_DIM
    wide = hps * hd
    ng = MOBA_HEADS // hps
    nb = seq // blk
    assert nb % (2 * MOBA_GROUP) == 0
    slope_b = jnp.broadcast_to((slopes.astype(F32) * LOG2E)[:, None, None], (MOBA_HEADS, 1, blk))
    whole_seq = lambda part: pl.BlockSpec((seq, wide), lambda b, hg, i: (b, part * ng + hg),
                                          pipeline_mode=pl.Buffered(1))
    return pl.pallas_call(
        _moba_kernel,
        grid=(batch, ng, nb),
        in_specs=[
            pl.BlockSpec((blk, wide), lambda b, hg, i: (b * nb + i, hg)),
            whole_seq(1),
            whole_seq(2),
            pl.BlockSpec((hps, 1, blk), lambda b, hg, i: (hg, 0, 0)),
        ],
        out_specs=pl.BlockSpec((blk, wide), lambda b, hg, i: (b * nb + i, hg)),
        out_shape=jax.ShapeDtypeStruct((m, MOBA_HEADS * hd), BF16),
        scratch_shapes=[
            pltpu.VMEM((hps, seq, 2 * hd), BF16),
            pltpu.VMEM((hps, blk, 2 * hd), BF16),
            pltpu.VMEM((hps, nb, hd), F32),
            pltpu.VMEM((hps, nb, blk), F32),
            pltpu.VMEM((hps * MOBA_GROUP, blk, blk), F32),
            pltpu.VMEM((hps * MOBA_GROUP, blk, blk), F32),
            pltpu.VMEM((hps, MOBA_VROWS, blk), F32),
            pltpu.VMEM((hps, nb, MOBA_VROWS, blk), BF16),
            pltpu.VMEM((hps * MOBA_GROUP, 1, blk), F32),
            pltpu.VMEM((hps * MOBA_GROUP, 1, blk), F32),
        ],
        compiler_params=_params(("parallel", "parallel", "arbitrary")),
        name="moba",
    )(proj_m, proj_m, proj_m, slope_b)


def _mem_attn_kernel(x_ref, gpre_ref, wq_ref, kv_ref, wo_ref, gpost_ref, o_ref):
    hd = HEAD_DIM
    width = MEM_HEADS * hd
    x = x_ref[...]
    hn = _rms(x, gpre_ref[...]).astype(BF16)
    q = (_dot(hn, wq_ref[...]) * (hd ** -0.5)).astype(BF16)
    kv = kv_ref[...]
    heads = []
    for hh in range(MEM_HEADS):
        s = _dot_nt(q[:, hh * hd:(hh + 1) * hd], kv[:, hh * hd:(hh + 1) * hd])
        p = jnp.exp(s - jnp.max(s, axis=-1, keepdims=True))
        p = p / jnp.sum(p, axis=-1, keepdims=True)
        heads.append(_dot(p.astype(BF16), kv[:, width + hh * hd:width + (hh + 1) * hd]))
    o = jnp.concatenate(heads, axis=-1).astype(BF16)
    o_ref[...] = x + _rms(_dot(o, wo_ref[...]), gpost_ref[...])


def _mem_attn(x2d, g_pre, wq_bf16, kv_bf16, wo_bf16, g_post, batch, seq, *, tm):
    m, d = x2d.shape
    nt = seq // tm
    mem_len, kvw = kv_bf16.shape[1], kv_bf16.shape[2]
    width = wq_bf16.shape[1]
    return pl.pallas_call(
        _mem_attn_kernel,
        grid=(batch, nt),
        in_specs=[
            pl.BlockSpec((tm, d), lambda b, t: (b * nt + t, 0)),
            pl.BlockSpec((1, d), lambda b, t: (0, 0)),
            pl.BlockSpec((d, width), lambda b, t: (0, 0)),
            pl.BlockSpec((None, mem_len, kvw), lambda b, t: (b, 0, 0)),
            pl.BlockSpec((width, d), lambda b, t: (0, 0)),
            pl.BlockSpec((1, d), lambda b, t: (0, 0)),
        ],
        out_specs=pl.BlockSpec((tm, d), lambda b, t: (b * nt + t, 0)),
        out_shape=jax.ShapeDtypeStruct((m, d), F32),
        compiler_params=_params(("parallel", "parallel")),
        name="mem_attn",
    )(x2d, g_pre.reshape(1, d), wq_bf16, kv_bf16, wo_bf16, g_post.reshape(1, d))


def kernel(x, mem, pre_mix_norm, w_in, conv_w, a_log, dt_bias, gdn_norm_w, w_out, post_mix_norm,
           pre_mem_norm, mem_kv_norm, w_mq, w_mk, w_mv, w_mo, post_mem_norm,
           pre_mlp_norm, w_up, w_down, post_mlp_norm):
    batch, seq, d = x.shape
    mem_len = mem.shape[1]
    gw = GDN_HEADS * HEAD_DIM
    mw = MOBA_HEADS * HEAD_DIM
    assert seq % MOBA_BLOCK == 0 and seq % 512 == 0
    x2d = x.reshape(batch * seq, d)
    mem2d = mem.reshape(batch * mem_len, d)
    slopes = jnp.exp2(-8.0 * jnp.arange(1, MOBA_HEADS + 1, dtype=F32) / MOBA_HEADS)
    ones = lambda n: jnp.ones((n,), F32)

    for l in range(w_in.shape[0]):
        win = w_in[l]
        w_g = win[:, :4 * gw].astype(BF16)
        w_ab = jnp.pad(win[:, 4 * gw:4 * gw + 2 * GDN_HEADS], ((0, 0), (0, HEAD_DIM - 2 * GDN_HEADS))).astype(BF16)
        w_m = win[:, 4 * gw + 2 * GDN_HEADS:].astype(BF16)
        q_scale = jnp.concatenate([jnp.full((mw,), LOG2E * HEAD_DIM ** -0.5, F32), ones(2 * mw)])

        proj_g, proj_ab = _norm_matmul(x2d, pre_mix_norm[l], w_g, ones(4 * gw), F32, tm=1024, tn=1024,
                                       side_w=w_ab, name="in_proj_gdn")
        proj_m = _norm_matmul(x2d, pre_mix_norm[l], w_m, q_scale, BF16, tm=1024, tn=1024, name="in_proj_moba")

        y_gdn = _gdn(proj_g, proj_ab, conv_w[l], a_log[l], dt_bias[l], gdn_norm_w[l], batch, seq, tt=256, hps=4)
        y_moba = _moba(proj_m, slopes, batch, seq, hps=2)
        x2d = _matmul_norm_res([y_gdn, y_moba], w_out[l].astype(BF16), post_mix_norm[l], x2d,
                               tm=512, tk=gw + mw, name="out_proj")

        w_kv = jnp.concatenate([w_mk[l], w_mv[l]], axis=1).astype(BF16)
        kv = _norm_matmul(mem2d, mem_kv_norm[l], w_kv, ones(w_kv.shape[1]), BF16, tm=512, tn=1024, name="mem_kv")
        x2d = _mem_attn(x2d, pre_mem_norm[l], w_mq[l].astype(BF16), kv.reshape(batch, mem_len, -1),
                        w_mo[l].astype(BF16), post_mem_norm[l], batch, seq, tm=512)

        hid = _norm_matmul(x2d, pre_mlp_norm[l], w_up[l].astype(BF16), ones(w_up.shape[2]), BF16,
                           tm=1024, tn=1024, relu2=True, name="mlp_up")
        x2d = _matmul_norm_res([hid], w_down[l].astype(BF16), post_mlp_norm[l], x2d, tm=512, tk=2048, name="mlp_down")
    return x2d.reshape(batch, seq, d)
```

```python
import functools
import math

import jax
import jax.numpy as jnp
from jax import lax
from jax.experimental import pallas as pl
from jax.experimental.pallas import tpu as pltpu

F32 = jnp.float32
BF16 = jnp.bfloat16

HEAD_DIM = 128
GDN_HEADS = 8
MOBA_HEADS = 8
CONV_WIDTH = 4
GDN_CHUNK = 64
GDN_SUB = 16
MOBA_BLOCK = 256
MOBA_TOPK = 3
MOBA_GROUP = 4
MOBA_VROWS = HEAD_DIM + 16
MEM_HEADS = 4
NORM_EPS = 1e-6
LOG2E = math.log2(math.e)

V7X_VMEM_BYTES = 64 * 1024 * 1024
VMEM_LIMIT = V7X_VMEM_BYTES - 8 * 1024 * 1024

NEG_INF = float("-inf")


def _params(semantics):
    return pltpu.CompilerParams(dimension_semantics=semantics, vmem_limit_bytes=VMEM_LIMIT)


def _dot(a, b):
    return jnp.dot(a, b, preferred_element_type=F32)


def _dot_nt(a, b):
    return lax.dot_general(a, b, (((1,), (1,)), ((), ())), preferred_element_type=F32)


def _dot_tn(a, b):
    return lax.dot_general(a, b, (((0,), (0,)), ((), ())), preferred_element_type=F32)


def _bdot(a, b):
    return _dot(a.astype(BF16), b.astype(BF16))


def _split3(x):
    hi = x.astype(BF16)
    rest = x - hi.astype(F32)
    mid = rest.astype(BF16)
    lo = (rest - mid.astype(F32)).astype(BF16)
    return hi, mid, lo


def _rms(y, gain):
    return y * lax.rsqrt(jnp.mean(y * y, axis=-1, keepdims=True) + NORM_EPS) * gain


def _norm_matmul_kernel(*refs, relu2, side):
    if side:
        x_ref, g_ref, w_ref, cs_ref, ws_ref, o_ref, os_ref, hn_ref = refs
    else:
        x_ref, g_ref, w_ref, cs_ref, o_ref, hn_ref = refs

    @pl.when(pl.program_id(1) == 0)
    def _():
        hn_ref[...] = _rms(x_ref[...], g_ref[...]).astype(BF16)
        if side:
            os_ref[...] = _dot(hn_ref[...], ws_ref[...])

    y = _dot(hn_ref[...], w_ref[...]) * cs_ref[...]
    if relu2:
        y = jnp.square(jnp.maximum(y, 0.0))
    o_ref[...] = y.astype(o_ref.dtype)


def _norm_matmul(x2d, gain, w_bf16, col_scale, out_dtype, *, tm, tn, relu2=False, side_w=None, name):
    m, d = x2d.shape
    n = w_bf16.shape[1]
    tm, tn = min(tm, m), min(tn, n)
    side = side_w is not None
    in_specs = [
        pl.BlockSpec((tm, d), lambda i, j: (i, 0)),
        pl.BlockSpec((1, d), lambda i, j: (0, 0)),
        pl.BlockSpec((d, tn), lambda i, j: (0, j)),
        pl.BlockSpec((1, tn), lambda i, j: (0, j)),
    ]
    out_specs = pl.BlockSpec((tm, tn), lambda i, j: (i, j))
    out_shape = jax.ShapeDtypeStruct((m, n), out_dtype)
    args = [x2d, gain.reshape(1, d), w_bf16, col_scale.reshape(1, n)]
    if side:
        ns = side_w.shape[1]
        in_specs.append(pl.BlockSpec((d, ns), lambda i, j: (0, 0)))
        out_specs = [out_specs, pl.BlockSpec((tm, ns), lambda i, j: (i, 0))]
        out_shape = [out_shape, jax.ShapeDtypeStruct((m, ns), F32)]
        args.append(side_w)
    return pl.pallas_call(
        functools.partial(_norm_matmul_kernel, relu2=relu2, side=side),
        grid=(m // tm, n // tn),
        in_specs=in_specs,
        out_specs=out_specs,
        out_shape=out_shape,
        scratch_shapes=[pltpu.VMEM((tm, d), BF16)],
        compiler_params=_params(("parallel", "arbitrary")),
        name=name,
    )(*args)


def _matmul_norm_res_kernel(*refs, n_in):
    a_refs = refs[:n_in]
    w_ref, g_ref, r_ref, o_ref, acc_ref = refs[n_in:]
    k = pl.program_id(1)

    @pl.when(k == 0)
    def _():
        acc_ref[...] = jnp.zeros_like(acc_ref)

    off = 0
    acc = acc_ref[...]
    for a_ref in a_refs:
        width = a_ref.shape[1]
        acc = acc + _dot(a_ref[...], w_ref[off:off + width, :])
        off += width
    acc_ref[...] = acc

    @pl.when(k == pl.num_programs(1) - 1)
    def _():
        o_ref[...] = r_ref[...] + _rms(acc_ref[...], g_ref[...])


def _matmul_norm_res(a_list, w_bf16, gain, resid, *, tm, tk, name):
    m, d = resid.shape
    ktot = w_bf16.shape[0]
    n_in = len(a_list)
    tm = min(tm, m)
    if n_in > 1:
        tk = ktot
        a_specs = [pl.BlockSpec((tm, a.shape[1]), lambda i, k: (i, 0)) for a in a_list]
    else:
        tk = min(tk, ktot)
        a_specs = [pl.BlockSpec((tm, tk), lambda i, k: (i, k))]
    return pl.pallas_call(
        functools.partial(_matmul_norm_res_kernel, n_in=n_in),
        grid=(m // tm, ktot // tk),
        in_specs=a_specs + [
            pl.BlockSpec((tk, d), lambda i, k: (k, 0)),
            pl.BlockSpec((1, d), lambda i, k: (0, 0)),
            pl.BlockSpec((tm, d), lambda i, k: (i, 0)),
        ],
        out_specs=pl.BlockSpec((tm, d), lambda i, k: (i, 0)),
        out_shape=jax.ShapeDtypeStruct((m, d), F32),
        scratch_shapes=[pltpu.VMEM((tm, d), F32)],
        compiler_params=_params(("parallel", "arbitrary")),
        name=name,
    )(*a_list, w_bf16, gain.reshape(1, d), resid)


def _gdn_kernel(q_ref, k_ref, v_ref, z_ref, ab_ref, cwq_ref, cwk_ref, cwv_ref, alog_ref, dtb_ref, nw_ref,
                o_ref, state_ref, tail_ref, pad_ref, qs_ref, ks_ref, vs_ref, gs_ref, bs_ref, os_ref,
                w2_ref, r_ref, qp_ref, op_ref, gl_ref):
    tt = q_ref.shape[0]
    c = GDN_CHUNK
    hd = HEAD_DIM
    hps = q_ref.shape[1] // hd
    n_chunks = tt // c
    head0 = pl.program_id(1) * hps
    lanes = lambda hh: slice(hh * hd, (hh + 1) * hd)

    @pl.when(pl.program_id(2) == 0)
    def _():
        state_ref[...] = jnp.zeros_like(state_ref)
        tail_ref[...] = jnp.zeros_like(tail_ref)

    def conv_silu(x_ref, slot, cw_ref):
        pad_ref[0:8, :] = tail_ref[slot]
        pad_ref[8:, :] = x_ref[...]
        tail_ref[slot] = x_ref[tt - 8:tt, :]
        y = cw_ref[0:1, :] * pad_ref[pl.ds(8 - (CONV_WIDTH - 1), tt), :]
        for j in range(1, CONV_WIDTH):
            y = y + cw_ref[j:j + 1, :] * pad_ref[pl.ds(8 - (CONV_WIDTH - 1) + j, tt), :]
        return y * jax.nn.sigmoid(y)

    def l2n(y):
        return y * lax.rsqrt(jnp.sum(y * y, axis=-1, keepdims=True) + NORM_EPS)

    qc = conv_silu(q_ref, 0, cwq_ref)
    kc = conv_silu(k_ref, 1, cwk_ref)
    vs_ref[...] = conv_silu(v_ref, 2, cwv_ref)
    ab = ab_ref[...]
    lane = lax.broadcasted_iota(jnp.int32, ab.shape, 1)
    xa = ab + dtb_ref[...]
    softplus = jnp.maximum(xa, 0.0) + jnp.log(1.0 + jnp.exp(-jnp.abs(xa)))
    g_all = -(jnp.exp(alog_ref[...]) * softplus)
    beta_all = jax.nn.sigmoid(ab)
    for hh in range(hps):
        qs_ref[:, lanes(hh)] = l2n(qc[:, lanes(hh)]) * (HEAD_DIM ** -0.5)
        ks_ref[:, lanes(hh)] = l2n(kc[:, lanes(hh)])
        g_col = jnp.sum(jnp.where(lane == head0 + hh, g_all, 0.0), axis=-1, keepdims=True)
        b_col = jnp.sum(jnp.where(lane == head0 + hh + GDN_HEADS, beta_all, 0.0), axis=-1, keepdims=True)
        gs_ref[:, lanes(hh)] = jnp.broadcast_to(g_col, (tt, hd))
        bs_ref[:, lanes(hh)] = jnp.broadcast_to(b_col, (tt, hd))

    row = lax.broadcasted_iota(jnp.int32, (c, c), 0)
    col = lax.broadcasted_iota(jnp.int32, (c, c), 1)
    causal = row >= col
    strict = row > col
    same_sub = (row // GDN_SUB) == (col // GDN_SUB)
    ltri3 = jnp.concatenate([causal.astype(BF16)] * 3, axis=1)

    units = [(n, hh) for n in range(n_chunks) for hh in range(hps)]
    each = lambda fn, *cols: [fn(*args) for args in zip(*cols)]
    rows = lambda ref: [ref[n * c:(n + 1) * c, lanes(hh)] for n, hh in units]
    q, k, v, g, beta = rows(qs_ref), rows(ks_ref), rows(vs_ref), rows(gs_ref), rows(bs_ref)
    pieces = each(_split3, g)
    gc = each(lambda p: _dot(ltri3, jnp.concatenate(p, axis=0)), pieces)
    dexp = each(lambda p: _dot(ltri3, jnp.concatenate([jnp.where(strict, x[:, :c], 0.0) for x in p], axis=0)), pieces)
    decay = each(lambda d: jnp.where(causal, jnp.exp(d), 0.0), dexp)
    kb = each(lambda a, b: a * b, k, beta)
    kbf = each(lambda a: a.astype(BF16), k)
    mm = each(lambda a, b, d: jnp.where(strict, _dot_nt(a.astype(BF16), b) * d, 0.0), kb, kbf, decay)
    a_qk = each(lambda a, b, d: (_dot_nt(a.astype(BF16), b) * d).astype(BF16), q, kbf, decay)
    x = each(lambda m_: jnp.where(same_sub, -m_, 0.0), mm)
    lo = each(lambda m_: jnp.where(same_sub, 0.0, m_), mm)
    x2 = each(_bdot, x, x)
    x4 = each(_bdot, x2, x2)
    a1 = each(lambda a, b: a + b + _bdot(a, b), x, x2)
    x8 = each(_bdot, x4, x4)
    a2 = each(lambda a, b: a + b + _bdot(a, b), a1, x4)
    dm = each(lambda a, b: a + b + _bdot(a, b), a2, x8)
    y = each(lambda d, l_: -(l_ + _bdot(d, l_)), dm, lo)
    y2 = each(_bdot, y, y)
    qm = each(lambda a, b: a + b + _bdot(a, b), y, y2)
    tm = each(lambda a, b: a + b + _bdot(a, b), qm, dm)
    eg = each(jnp.exp, gc)
    uw = each(lambda v_, b_, kb_, e_: jnp.concatenate([v_ * b_, kb_ * e_], axis=1), v, beta, kb, eg)
    uw = each(lambda t_, u_: (u_ + _bdot(t_, u_)).astype(BF16), tm, uw)
    au = each(_dot, a_qk, uw)
    kt = each(lambda k_, g_: (k_ * jnp.exp(g_[c - 1:c, :] - g_)).astype(BF16), k, gc)
    kw = each(_dot_tn, kt, uw)
    for u in range(len(units)):
        op_ref[u] = au[u][:, :hd]
        qp_ref[u] = (q[u] * eg[u] - au[u][:, hd:]).astype(BF16)
        r_ref[u] = kw[u][:, :hd]
        w2_ref[u] = kw[u][:, hd:].astype(BF16)
        gl_ref[u] = jnp.exp(gc[u][c - 1:c, :])

    s = [state_ref[hh] for hh in range(hps)]
    for u, (n, hh) in enumerate(units):
        sb = s[hh].astype(BF16)
        os_ref[n * c:(n + 1) * c, lanes(hh)] = op_ref[u] + _dot(qp_ref[u], sb)
        s[hh] = s[hh] * gl_ref[u] + r_ref[u] - _dot(w2_ref[u], sb)
    for hh in range(hps):
        state_ref[hh] = s[hh]
        z = z_ref[:, lanes(hh)]
        o_ref[:, lanes(hh)] = (_rms(os_ref[:, lanes(hh)], nw_ref[...]) * (z * jax.nn.sigmoid(z))).astype(o_ref.dtype)


def _gdn(proj_g, proj_ab, conv_w, a_log, dt_bias, norm_w, batch, seq, *, tt, hps):
    m = proj_g.shape[0]
    nt = seq // tt
    hd = HEAD_DIM
    wide = hps * hd
    ng = GDN_HEADS // hps
    n_units = (tt // GDN_CHUNK) * hps
    row = lambda b, hg, t: b * nt + t
    on_a_lanes = lambda p: jnp.pad(p.astype(F32), (0, hd - GDN_HEADS)).reshape(1, hd)
    tile = lambda part: pl.BlockSpec((tt, wide), lambda b, hg, t: (row(b, hg, t), part * ng + hg))
    cw = lambda part: pl.BlockSpec((CONV_WIDTH, wide), lambda b, hg, t: (0, part * ng + hg))
    per_head = pl.BlockSpec((1, hd), lambda b, hg, t: (0, 0))
    return pl.pallas_call(
        _gdn_kernel,
        grid=(batch, ng, nt),
        in_specs=[
            tile(0), tile(1), tile(2), tile(3),
            pl.BlockSpec((tt, hd), lambda b, hg, t: (row(b, hg, t), 0)),
            cw(0), cw(1), cw(2),
            per_head, per_head,
            pl.BlockSpec((1, hd), lambda b, hg, t: (0, 0)),
        ],
        out_specs=pl.BlockSpec((tt, wide), lambda b, hg, t: (row(b, hg, t), hg)),
        out_shape=jax.ShapeDtypeStruct((m, GDN_HEADS * hd), BF16),
        scratch_shapes=[
            pltpu.VMEM((hps, hd, hd), F32),
            pltpu.VMEM((3, 8, wide), F32),
            pltpu.VMEM((tt + 8, wide), F32),
            pltpu.VMEM((tt, wide), F32),
            pltpu.VMEM((tt, wide), F32),
            pltpu.VMEM((tt, wide), F32),
            pltpu.VMEM((tt, wide), F32),
            pltpu.VMEM((tt, wide), F32),
            pltpu.VMEM((tt, wide), F32),
            pltpu.VMEM((n_units, hd, hd), BF16),
            pltpu.VMEM((n_units, hd, hd), F32),
            pltpu.VMEM((n_units, GDN_CHUNK, hd), BF16),
            pltpu.VMEM((n_units, GDN_CHUNK, hd), F32),
            pltpu.VMEM((n_units, 1, hd), F32),
        ],
        compiler_params=_params(("parallel", "parallel", "arbitrary")),
        name="gdn",
    )(proj_g, proj_g, proj_g, proj_g, proj_ab, conv_w, conv_w, conv_w,
      on_a_lanes(a_log), on_a_lanes(dt_bias), norm_w.reshape(1, hd))


def _moba_kernel(q_ref, k_ref, v_ref, slope_ref, o_ref, kaug_ref, qaug_ref, kmean_ref, kparts_ref, sel_ref, sa_ref, sb_ref, acc_ref, vt_ref,
                 ta_ref, tb_ref):
    blk = MOBA_BLOCK
    hd = HEAD_DIM
    grp = MOBA_GROUP
    hps = q_ref.shape[1] // hd
    heads = range(hps)
    nb = k_ref.shape[0] // blk
    i = pl.program_id(2)
    lanes = lambda hh: slice(hh * hd, (hh + 1) * hd)
    slope2 = [slope_ref[hh] for hh in heads]

    @pl.when(i == 0)
    def _():
        pos = lax.broadcasted_iota(jnp.int32, (blk, hd), 0).astype(F32)
        lane = lax.broadcasted_iota(jnp.int32, (blk, hd), 1)
        one = jnp.ones((blk, hd), BF16)
        zero = jnp.zeros((blk, hd), BF16)
        ones_row = (lax.broadcasted_iota(jnp.int32, (MOBA_VROWS - hd, blk), 0) == 0).astype(BF16)
        k_extra = []
        for hh in heads:
            hi, mid, lo = _split3(slope2[hh][:, :hd] * pos)
            pieces = jnp.where(lane == 0, hi, jnp.where(lane == 1, mid, jnp.where(lane == 2, lo, zero)))
            k_extra.append(jnp.where((lane >= 3) & (lane < 6), one, pieces))
            npieces = jnp.where(lane == 3, -hi, jnp.where(lane == 4, -mid, jnp.where(lane == 5, -lo, zero)))
            qaug_ref[hh, :, hd:] = jnp.where(lane < 3, one, npieces)

        def block_step(j, carry):
            r0 = pl.multiple_of(j * blk, blk)
            for hh in heads:
                kb = k_ref[pl.ds(r0, blk), lanes(hh)]
                kaug_ref[hh, pl.ds(r0, blk), 0:hd] = kb
                kaug_ref[hh, pl.ds(r0, blk), hd:] = k_extra[hh]
                kmean_ref[hh, pl.ds(j, 1), :] = jnp.mean(kb.astype(F32), axis=0, keepdims=True)
                vt_ref[hh, j, 0:hd, :] = v_ref[pl.ds(r0, blk), lanes(hh)].astype(F32).T.astype(BF16)
                vt_ref[hh, j, hd:, :] = ones_row
            return carry
        lax.fori_loop(0, nb, block_step, 0)
        for hh in heads:
            for part, piece in enumerate(_split3(kmean_ref[hh])):
                kparts_ref[hh, part * nb:(part + 1) * nb, :] = piece

    blk_id = lax.broadcasted_iota(jnp.int32, (nb, blk), 0)
    key_pos = lax.broadcasted_iota(jnp.int32, (blk, blk), 0)
    qry_pos = lax.broadcasted_iota(jnp.int32, (blk, blk), 1)
    r0 = pl.multiple_of(i * blk, blk)
    def score_group(g, dst_ref, top_ref):
        for b in range(grp):
            c0 = pl.multiple_of((g * grp + b) * blk, blk)
            for hh in heads:
                s2 = _dot_nt(kaug_ref[hh, pl.ds(c0, blk), :], qa[hh])
                dst_ref[hh * grp + b] = s2
                top_ref[hh * grp + b] = jnp.max(s2, axis=0, keepdims=True)

    qa, gates, own = [], [], []
    for hh in heads:
        q = q_ref[:, lanes(hh)]
        qaug_ref[hh, :, 0:hd] = q
        qa.append(qaug_ref[hh])
        g3 = _dot_nt(kparts_ref[hh], q)
        gates.append(g3[0:nb] + g3[nb:2 * nb] + g3[2 * nb:3 * nb])
        own.append(_dot_nt(kaug_ref[hh, pl.ds(r0, blk), :], qa[hh]))
    score_group(0, sa_ref, ta_ref)
    score_group(1, sb_ref, tb_ref)

    start = []
    for hh in heads:
        gate = jnp.where(blk_id < i, gates[hh], NEG_INF)
        sel = jnp.zeros((nb, blk), F32)
        for _ in range(MOBA_TOPK):
            best = jnp.max(gate, axis=0, keepdims=True)
            hit = (gate == best) & (best > NEG_INF)
            first = jnp.min(jnp.where(hit, blk_id, nb), axis=0, keepdims=True)
            pick = blk_id == first
            sel = jnp.where(pick, 1.0, sel)
            gate = jnp.where(pick, NEG_INF, gate)
        sel_ref[hh] = sel
        s = jnp.where(key_pos <= qry_pos, own[hh], NEG_INF)
        m0 = jnp.max(s, axis=0, keepdims=True)
        p = jnp.exp2(s - m0)
        acc_ref[hh] = _dot(vt_ref[hh, i], p.astype(BF16))
        start.append(m0)

    def group_max(g, top_ref, carry):
        subs = [[] for _ in heads]
        m_new = [carry[hh] for hh in heads]
        for b in range(grp):
            j = g * grp + b
            for hh in heads:
                chosen = sel_ref[hh, pl.ds(j, 1), :] > 0.0
                off = slope2[hh] * ((i - j) * blk).astype(F32)
                top = top_ref[hh * grp + b] - off
                m_new[hh] = jnp.maximum(m_new[hh], jnp.where(chosen, top, NEG_INF))
                subs[hh].append((chosen, off))
        return m_new, subs

    last_group = nb // grp - 1
    n_groups = (i + grp - 1) // grp
    n_pairs = n_groups // 2

    def attend_group(g, src_ref, top_ref, carry, refill):
        m_new, subs = group_max(g, top_ref, carry)
        alpha = [jnp.exp2(carry[hh] - m_new[hh]) for hh in heads]
        nxt = jnp.minimum(g + 2, last_group)
        pv = [None for _ in heads]
        for b in range(grp):
            c0 = pl.multiple_of((nxt * grp + b) * blk, blk)
            for hh in heads:
                chosen, off = subs[hh][b]
                p2 = jnp.exp2((src_ref[hh * grp + b] - jnp.where(chosen, m_new[hh] + off, jnp.inf)).astype(BF16))
                part = _dot(vt_ref[hh, g * grp + b], p2)
                pv[hh] = part if pv[hh] is None else pv[hh] + part
                if refill:
                    s2 = _dot_nt(kaug_ref[hh, pl.ds(c0, blk), :], qa[hh])
                    src_ref[hh * grp + b] = s2
                    top_ref[hh * grp + b] = jnp.max(s2, axis=0, keepdims=True)
        for hh in heads:
            acc_ref[hh] = acc_ref[hh] * alpha[hh] + pv[hh]
        return tuple(m_new)

    def past_pair(gp, carry, refill_odd=True):
        carry = attend_group(2 * gp, sa_ref, ta_ref, carry, True)
        return attend_group(2 * gp + 1, sb_ref, tb_ref, carry, refill_odd)

    final = lax.fori_loop(0, jnp.maximum(n_pairs - 1, 0), past_pair, tuple(start))
    final = lax.cond(n_pairs > 0, lambda c: past_pair(n_pairs - 1, c, False), lambda c: c, final)

    @pl.when(n_groups % 2 == 1)
    def _():
        attend_group(n_groups - 1, sa_ref, ta_ref, final, False)

    for hh in heads:
        o_ref[:, lanes(hh)] = (acc_ref[hh, 0:hd, :] / acc_ref[hh, hd:hd + 1, :]).T.astype(o_ref.dtype)


def _moba(proj_m, slopes, batch, seq, *, hps):
    m = proj_m.shape[0]
    blk, hd = MOBA_BLOCK, HEAD_DIM
    wide = hps * hd
    ng = MOBA_HEADS // hps
    nb = seq // blk
    assert nb % (2 * MOBA_GROUP) == 0
    slope_b = jnp.broadcast_to((slopes.astype(F32) * LOG2E)[:, None, None], (MOBA_HEADS, 1, blk))
    whole_seq = lambda part: pl.BlockSpec((seq, wide), lambda b, hg, i: (b, part * ng + hg),
                                          pipeline_mode=pl.Buffered(1))
    return pl.pallas_call(
        _moba_kernel,
        grid=(batch, ng, nb),
        in_specs=[
            pl.BlockSpec((blk, wide), lambda b, hg, i: (b * nb + i, hg)),
            whole_seq(1),
            whole_seq(2),
            pl.BlockSpec((hps, 1, blk), lambda b, hg, i: (hg, 0, 0)),
        ],
        out_specs=pl.BlockSpec((blk, wide), lambda b, hg, i: (b * nb + i, hg)),
        out_shape=jax.ShapeDtypeStruct((m, MOBA_HEADS * hd), BF16),
        scratch_shapes=[
            pltpu.VMEM((hps, seq, 2 * hd), BF16),
            pltpu.VMEM((hps, blk, 2 * hd), BF16),
            pltpu.VMEM((hps, nb, hd), F32),
            pltpu.VMEM((hps, 3 * nb, hd), BF16),
            pltpu.VMEM((hps, nb, blk), F32),
            pltpu.VMEM((hps * MOBA_GROUP, blk, blk), F32),
            pltpu.VMEM((hps * MOBA_GROUP, blk, blk), F32),
            pltpu.VMEM((hps, MOBA_VROWS, blk), F32),
            pltpu.VMEM((hps, nb, MOBA_VROWS, blk), BF16),
            pltpu.VMEM((hps * MOBA_GROUP, 1, blk), F32),
            pltpu.VMEM((hps * MOBA_GROUP, 1, blk), F32),
        ],
        compiler_params=_params(("parallel", "parallel", "arbitrary")),
        name="moba",
    )(proj_m, proj_m, proj_m, slope_b)


def _mem_attn_kernel(x_ref, gpre_ref, wq_ref, kv_ref, wo_ref, gpost_ref, o_ref):
    hd = HEAD_DIM
    width = MEM_HEADS * hd
    x = x_ref[...]
    hn = _rms(x, gpre_ref[...]).astype(BF16)
    q = (_dot(hn, wq_ref[...]) * (hd ** -0.5)).astype(BF16)
    kv = kv_ref[...]
    heads = []
    for hh in range(MEM_HEADS):
        s = _dot_nt(q[:, hh * hd:(hh + 1) * hd], kv[:, hh * hd:(hh + 1) * hd])
        p = jnp.exp(s - jnp.max(s, axis=-1, keepdims=True))
        p = p / jnp.sum(p, axis=-1, keepdims=True)
        heads.append(_dot(p.astype(BF16), kv[:, width + hh * hd:width + (hh + 1) * hd]))
    o = jnp.concatenate(heads, axis=-1).astype(BF16)
    o_ref[...] = x + _rms(_dot(o, wo_ref[...]), gpost_ref[...])


def _mem_attn(x2d, g_pre, wq_bf16, kv_bf16, wo_bf16, g_post, batch, seq, *, tm):
    m, d = x2d.shape
    nt = seq // tm
    mem_len, kvw = kv_bf16.shape[1], kv_bf16.shape[2]
    width = wq_bf16.shape[1]
    return pl.pallas_call(
        _mem_attn_kernel,
        grid=(batch, nt),
        in_specs=[
            pl.BlockSpec((tm, d), lambda b, t: (b * nt + t, 0)),
            pl.BlockSpec((1, d), lambda b, t: (0, 0)),
            pl.BlockSpec((d, width), lambda b, t: (0, 0)),
            pl.BlockSpec((None, mem_len, kvw), lambda b, t: (b, 0, 0)),
            pl.BlockSpec((width, d), lambda b, t: (0, 0)),
            pl.BlockSpec((1, d), lambda b, t: (0, 0)),
        ],
        out_specs=pl.BlockSpec((tm, d), lambda b, t: (b * nt + t, 0)),
        out_shape=jax.ShapeDtypeStruct((m, d), F32),
        compiler_params=_params(("parallel", "parallel")),
        name="mem_attn",
    )(x2d, g_pre.reshape(1, d), wq_bf16, kv_bf16, wo_bf16, g_post.reshape(1, d))


def kernel(x, mem, pre_mix_norm, w_in, conv_w, a_log, dt_bias, gdn_norm_w, w_out, post_mix_norm,
           pre_mem_norm, mem_kv_norm, w_mq, w_mk, w_mv, w_mo, post_mem_norm,
           pre_mlp_norm, w_up, w_down, post_mlp_norm):
    batch, seq, d = x.shape
    mem_len = mem.shape[1]
    gw = GDN_HEADS * HEAD_DIM
    mw = MOBA_HEADS * HEAD_DIM
    assert seq % MOBA_BLOCK == 0 and seq % 512 == 0
    x2d = x.reshape(batch * seq, d)
    mem2d = mem.reshape(batch * mem_len, d)
    slopes = jnp.exp2(-8.0 * jnp.arange(1, MOBA_HEADS + 1, dtype=F32) / MOBA_HEADS)
    ones = lambda n: jnp.ones((n,), F32)

    for l in range(w_in.shape[0]):
        win = w_in[l]
        w_g = win[:, :4 * gw].astype(BF16)
        w_ab = jnp.pad(win[:, 4 * gw:4 * gw + 2 * GDN_HEADS], ((0, 0), (0, HEAD_DIM - 2 * GDN_HEADS))).astype(BF16)
        w_m = win[:, 4 * gw + 2 * GDN_HEADS:].astype(BF16)
        q_scale = jnp.concatenate([jnp.full((mw,), LOG2E * HEAD_DIM ** -0.5, F32), ones(2 * mw)])

        proj_g, proj_ab = _norm_matmul(x2d, pre_mix_norm[l], w_g, ones(4 * gw), F32, tm=1024, tn=1024,
                                       side_w=w_ab, name="in_proj_gdn")
        proj_m = _norm_matmul(x2d, pre_mix_norm[l], w_m, q_scale, BF16, tm=1024, tn=1024, name="in_proj_moba")

        y_gdn = _gdn(proj_g, proj_ab, conv_w[l], a_log[l], dt_bias[l], gdn_norm_w[l], batch, seq, tt=256, hps=4)
        y_moba = _moba(proj_m, slopes, batch, seq, hps=2)
        x2d = _matmul_norm_res([y_gdn, y_moba], w_out[l].astype(BF16), post_mix_norm[l], x2d,
                               tm=512, tk=gw + mw, name="out_proj")

        w_kv = jnp.concatenate([w_mk[l], w_mv[l]], axis=1).astype(BF16)
        kv = _norm_matmul(mem2d, mem_kv_norm[l], w_kv, ones(w_kv.shape[1]), BF16, tm=512, tn=1024, name="mem_kv")
        x2d = _mem_attn(x2d, pre_mem_norm[l], w_mq[l].astype(BF16), kv.reshape(batch, mem_len, -1),
                        w_mo[l].astype(BF16), post_mem_norm[l], batch, seq, tm=512)

        hid = _norm_matmul(x2d, pre_mlp_norm[l], w_up[l].astype(BF16), ones(w_up.shape[2]), BF16,
                           tm=1024, tn=1024, relu2=True, name="mlp_up")
        x2d = _matmul_norm_res([hid], w_down[l].astype(BF16), post_mlp_norm[l], x2d, tm=512, tk=2048, name="mlp_down")
    return x2d.reshape(batch, seq, d)
```

```python
import functools
import math

import jax
import jax.numpy as jnp
from jax import lax
from jax.experimental import pallas as pl
from jax.experimental.pallas import tpu as pltpu

F32 = jnp.float32
BF16 = jnp.bfloat16

HEAD_DIM = 128
GDN_HEADS = 8
MOBA_HEADS = 8
CONV_WIDTH = 4
GDN_CHUNK = 64
GDN_SUB = 16
MOBA_BLOCK = 256
MOBA_TOPK = 3
MOBA_GROUP = 4
MOBA_VROWS = HEAD_DIM + 16
MEM_HEADS = 4
NORM_EPS = 1e-6
NORM_CHUNKS = 4
LOG2E = math.log2(math.e)

V7X_VMEM_BYTES = 64 * 1024 * 1024
VMEM_LIMIT = V7X_VMEM_BYTES - 8 * 1024 * 1024

NEG_INF = float("-inf")


def _params(semantics):
    return pltpu.CompilerParams(dimension_semantics=semantics, vmem_limit_bytes=VMEM_LIMIT)


def _dot(a, b):
    return jnp.dot(a, b, preferred_element_type=F32)


def _dot_nt(a, b):
    return lax.dot_general(a, b, (((1,), (1,)), ((), ())), preferred_element_type=F32)


def _dot_tn(a, b):
    return lax.dot_general(a, b, (((0,), (0,)), ((), ())), preferred_element_type=F32)


def _bdot(a, b):
    return _dot(a.astype(BF16), b.astype(BF16))


def _split3(x):
    hi = x.astype(BF16)
    rest = x - hi.astype(F32)
    mid = rest.astype(BF16)
    lo = (rest - mid.astype(F32)).astype(BF16)
    return hi, mid, lo


def _rms(y, gain):
    return y * lax.rsqrt(jnp.mean(y * y, axis=-1, keepdims=True) + NORM_EPS) * gain


def _norm_matmul_kernel(*refs, relu2, side):
    if side:
        x_ref, g_ref, w_ref, cs_ref, ws_ref, o_ref, os_ref, hn_ref = refs
    else:
        x_ref, g_ref, w_ref, cs_ref, o_ref, hn_ref = refs

    def project(hn, rows):
        y = _dot(hn, w_ref[...]) * cs_ref[...]
        if relu2:
            y = jnp.square(jnp.maximum(y, 0.0))
        o_ref[rows, :] = y.astype(o_ref.dtype)

    @pl.when(pl.program_id(1) == 0)
    def _():
        step = x_ref.shape[0] // NORM_CHUNKS
        for r in range(NORM_CHUNKS):
            rows = slice(r * step, (r + 1) * step)
            hn = _rms(x_ref[rows, :], g_ref[...]).astype(BF16)
            hn_ref[rows, :] = hn
            if side:
                os_ref[rows, :] = _dot(hn, ws_ref[...])
            project(hn, rows)

    @pl.when(pl.program_id(1) != 0)
    def _():
        project(hn_ref[...], slice(None))


def _norm_matmul(x2d, gain, w_bf16, col_scale, out_dtype, *, tm, tn, relu2=False, side_w=None, name):
    m, d = x2d.shape
    n = w_bf16.shape[1]
    tm, tn = min(tm, m), min(tn, n)
    side = side_w is not None
    in_specs = [
        pl.BlockSpec((tm, d), lambda i, j: (i, 0)),
        pl.BlockSpec((1, d), lambda i, j: (0, 0)),
        pl.BlockSpec((d, tn), lambda i, j: (0, j)),
        pl.BlockSpec((1, tn), lambda i, j: (0, j)),
    ]
    out_specs = pl.BlockSpec((tm, tn), lambda i, j: (i, j))
    out_shape = jax.ShapeDtypeStruct((m, n), out_dtype)
    args = [x2d, gain.reshape(1, d), w_bf16, col_scale.reshape(1, n)]
    if side:
        ns = side_w.shape[1]
        in_specs.append(pl.BlockSpec((d, ns), lambda i, j: (0, 0)))
        out_specs = [out_specs, pl.BlockSpec((tm, ns), lambda i, j: (i, 0))]
        out_shape = [out_shape, jax.ShapeDtypeStruct((m, ns), F32)]
        args.append(side_w)
    return pl.pallas_call(
        functools.partial(_norm_matmul_kernel, relu2=relu2, side=side),
        grid=(m // tm, n // tn),
        in_specs=in_specs,
        out_specs=out_specs,
        out_shape=out_shape,
        scratch_shapes=[pltpu.VMEM((tm, d), BF16)],
        compiler_params=_params(("parallel", "arbitrary")),
        name=name,
    )(*args)


def _matmul_norm_res_kernel(*refs, n_in):
    a_refs = refs[:n_in]
    w_ref, g_ref, r_ref, o_ref, acc_ref = refs[n_in:]
    k = pl.program_id(1)

    @pl.when(k == 0)
    def _():
        acc_ref[...] = jnp.zeros_like(acc_ref)

    off = 0
    acc = acc_ref[...]
    for a_ref in a_refs:
        width = a_ref.shape[1]
        acc = acc + _dot(a_ref[...], w_ref[off:off + width, :])
        off += width
    acc_ref[...] = acc

    @pl.when(k == pl.num_programs(1) - 1)
    def _():
        o_ref[...] = r_ref[...] + _rms(acc_ref[...], g_ref[...])


def _matmul_norm_res(a_list, w_bf16, gain, resid, *, tm, tk, name):
    m, d = resid.shape
    ktot = w_bf16.shape[0]
    n_in = len(a_list)
    tm = min(tm, m)
    if n_in > 1:
        tk = ktot
        a_specs = [pl.BlockSpec((tm, a.shape[1]), lambda i, k: (i, 0)) for a in a_list]
    else:
        tk = min(tk, ktot)
        a_specs = [pl.BlockSpec((tm, tk), lambda i, k: (i, k))]
    return pl.pallas_call(
        functools.partial(_matmul_norm_res_kernel, n_in=n_in),
        grid=(m // tm, ktot // tk),
        in_specs=a_specs + [
            pl.BlockSpec((tk, d), lambda i, k: (k, 0)),
            pl.BlockSpec((1, d), lambda i, k: (0, 0)),
            pl.BlockSpec((tm, d), lambda i, k: (i, 0)),
        ],
        out_specs=pl.BlockSpec((tm, d), lambda i, k: (i, 0)),
        out_shape=jax.ShapeDtypeStruct((m, d), F32),
        scratch_shapes=[pltpu.VMEM((tm, d), F32)],
        compiler_params=_params(("parallel", "arbitrary")),
        name=name,
    )(*a_list, w_bf16, gain.reshape(1, d), resid)


def _gdn_kernel(q_ref, k_ref, v_ref, z_ref, ab_ref, cwq_ref, cwk_ref, cwv_ref, alog_ref, dtb_ref, nw_ref,
                o_ref, state_ref, tail_ref, pad_ref, qs_ref, ks_ref, vs_ref, gs_ref, bs_ref, os_ref,
                w2_ref, r_ref, qp_ref, op_ref, gl_ref):
    tt = q_ref.shape[0]
    c = GDN_CHUNK
    hd = HEAD_DIM
    hps = q_ref.shape[1] // hd
    n_chunks = tt // c
    head0 = pl.program_id(1) * hps
    lanes = lambda hh: slice(hh * hd, (hh + 1) * hd)

    @pl.when(pl.program_id(2) == 0)
    def _():
        state_ref[...] = jnp.zeros_like(state_ref)
        tail_ref[...] = jnp.zeros_like(tail_ref)

    def conv_silu(x_ref, slot, cw_ref):
        pad_ref[0:8, :] = tail_ref[slot]
        pad_ref[8:, :] = x_ref[...]
        tail_ref[slot] = x_ref[tt - 8:tt, :]
        y = cw_ref[0:1, :] * pad_ref[pl.ds(8 - (CONV_WIDTH - 1), tt), :]
        for j in range(1, CONV_WIDTH):
            y = y + cw_ref[j:j + 1, :] * pad_ref[pl.ds(8 - (CONV_WIDTH - 1) + j, tt), :]
        return y * jax.nn.sigmoid(y)

    def l2n(y):
        return y * lax.rsqrt(jnp.sum(y * y, axis=-1, keepdims=True) + NORM_EPS)

    qc = conv_silu(q_ref, 0, cwq_ref)
    kc = conv_silu(k_ref, 1, cwk_ref)
    vs_ref[...] = conv_silu(v_ref, 2, cwv_ref)
    ab = ab_ref[...]
    lane = lax.broadcasted_iota(jnp.int32, ab.shape, 1)
    xa = ab + dtb_ref[...]
    softplus = jnp.maximum(xa, 0.0) + jnp.log(1.0 + jnp.exp(-jnp.abs(xa)))
    g_all = -(jnp.exp(alog_ref[...]) * softplus)
    beta_all = jax.nn.sigmoid(ab)
    for hh in range(hps):
        qs_ref[:, lanes(hh)] = l2n(qc[:, lanes(hh)]) * (HEAD_DIM ** -0.5)
        ks_ref[:, lanes(hh)] = l2n(kc[:, lanes(hh)])
        g_col = jnp.sum(jnp.where(lane == head0 + hh, g_all, 0.0), axis=-1, keepdims=True)
        b_col = jnp.sum(jnp.where(lane == head0 + hh + GDN_HEADS, beta_all, 0.0), axis=-1, keepdims=True)
        gs_ref[:, lanes(hh)] = jnp.broadcast_to(g_col, (tt, hd))
        bs_ref[:, lanes(hh)] = jnp.broadcast_to(b_col, (tt, hd))

    row = lax.broadcasted_iota(jnp.int32, (c, c), 0)
    col = lax.broadcasted_iota(jnp.int32, (c, c), 1)
    causal = row >= col
    strict = row > col
    same_sub = (row // GDN_SUB) == (col // GDN_SUB)
    ltri3 = jnp.concatenate([causal.astype(BF16)] * 3, axis=1)

    units = [(n, hh) for n in range(n_chunks) for hh in range(hps)]
    each = lambda fn, *cols: [fn(*args) for args in zip(*cols)]
    rows = lambda ref: [ref[n * c:(n + 1) * c, lanes(hh)] for n, hh in units]
    q, k, v, g, beta = rows(qs_ref), rows(ks_ref), rows(vs_ref), rows(gs_ref), rows(bs_ref)
    pieces = each(_split3, g)
    gc = each(lambda p: _dot(ltri3, jnp.concatenate(p, axis=0)), pieces)
    dexp = each(lambda p: _dot(ltri3, jnp.concatenate([jnp.where(strict, x[:, :c], 0.0) for x in p], axis=0)), pieces)
    decay = each(lambda d: jnp.where(causal, jnp.exp(d), 0.0), dexp)
    kb = each(lambda a, b: a * b, k, beta)
    kbf = each(lambda a: a.astype(BF16), k)
    mm = each(lambda a, b, d: jnp.where(strict, _dot_nt(a.astype(BF16), b) * d, 0.0), kb, kbf, decay)
    a_qk = each(lambda a, b, d: (_dot_nt(a.astype(BF16), b) * d).astype(BF16), q, kbf, decay)
    x = each(lambda m_: jnp.where(same_sub, -m_, 0.0), mm)
    lo = each(lambda m_: jnp.where(same_sub, 0.0, m_), mm)
    x2 = each(_bdot, x, x)
    x4 = each(_bdot, x2, x2)
    a1 = each(lambda a, b: a + b + _bdot(a, b), x, x2)
    x8 = each(_bdot, x4, x4)
    a2 = each(lambda a, b: a + b + _bdot(a, b), a1, x4)
    dm = each(lambda a, b: a + b + _bdot(a, b), a2, x8)
    y = each(lambda d, l_: -(l_ + _bdot(d, l_)), dm, lo)
    y2 = each(_bdot, y, y)
    qm = each(lambda a, b: a + b + _bdot(a, b), y, y2)
    tm = each(lambda a, b: a + b + _bdot(a, b), qm, dm)
    eg = each(jnp.exp, gc)
    uw = each(lambda v_, b_, kb_, e_: jnp.concatenate([v_ * b_, kb_ * e_], axis=1), v, beta, kb, eg)
    uw = each(lambda t_, u_: (u_ + _bdot(t_, u_)).astype(BF16), tm, uw)
    au = each(_dot, a_qk, uw)
    kt = each(lambda k_, g_: (k_ * jnp.exp(g_[c - 1:c, :] - g_)).astype(BF16), k, gc)
    kw = each(_dot_tn, kt, uw)
    for u in range(len(units)):
        op_ref[u] = au[u][:, :hd]
        qp_ref[u] = (q[u] * eg[u] - au[u][:, hd:]).astype(BF16)
        r_ref[u] = kw[u][:, :hd]
        w2_ref[u] = kw[u][:, hd:].astype(BF16)
        gl_ref[u] = jnp.exp(gc[u][c - 1:c, :])

    s = [state_ref[hh] for hh in range(hps)]
    for u, (n, hh) in enumerate(units):
        sb = s[hh].astype(BF16)
        os_ref[n * c:(n + 1) * c, lanes(hh)] = op_ref[u] + _dot(qp_ref[u], sb)
        s[hh] = s[hh] * gl_ref[u] + r_ref[u] - _dot(w2_ref[u], sb)
    for hh in range(hps):
        state_ref[hh] = s[hh]
        z = z_ref[:, lanes(hh)]
        o_ref[:, lanes(hh)] = (_rms(os_ref[:, lanes(hh)], nw_ref[...]) * (z * jax.nn.sigmoid(z))).astype(o_ref.dtype)


def _gdn(proj_g, proj_ab, conv_w, a_log, dt_bias, norm_w, batch, seq, *, tt, hps):
    m = proj_g.shape[0]
    nt = seq // tt
    hd = HEAD_DIM
    wide = hps * hd
    ng = GDN_HEADS // hps
    n_units = (tt // GDN_CHUNK) * hps
    row = lambda b, hg, t: b * nt + t
    on_a_lanes = lambda p: jnp.pad(p.astype(F32), (0, hd - GDN_HEADS)).reshape(1, hd)
    tile = lambda part: pl.BlockSpec((tt, wide), lambda b, hg, t: (row(b, hg, t), part * ng + hg))
    cw = lambda part: pl.BlockSpec((CONV_WIDTH, wide), lambda b, hg, t: (0, part * ng + hg))
    per_head = pl.BlockSpec((1, hd), lambda b, hg, t: (0, 0))
    return pl.pallas_call(
        _gdn_kernel,
        grid=(batch, ng, nt),
        in_specs=[
            tile(0), tile(1), tile(2), tile(3),
            pl.BlockSpec((tt, hd), lambda b, hg, t: (row(b, hg, t), 0)),
            cw(0), cw(1), cw(2),
            per_head, per_head,
            pl.BlockSpec((1, hd), lambda b, hg, t: (0, 0)),
        ],
        out_specs=pl.BlockSpec((tt, wide), lambda b, hg, t: (row(b, hg, t), hg)),
        out_shape=jax.ShapeDtypeStruct((m, GDN_HEADS * hd), BF16),
        scratch_shapes=[
            pltpu.VMEM((hps, hd, hd), F32),
            pltpu.VMEM((3, 8, wide), F32),
            pltpu.VMEM((tt + 8, wide), F32),
            pltpu.VMEM((tt, wide), F32),
            pltpu.VMEM((tt, wide), F32),
            pltpu.VMEM((tt, wide), F32),
            pltpu.VMEM((tt, wide), F32),
            pltpu.VMEM((tt, wide), F32),
            pltpu.VMEM((tt, wide), F32),
            pltpu.VMEM((n_units, hd, hd), BF16),
            pltpu.VMEM((n_units, hd, hd), F32),
            pltpu.VMEM((n_units, GDN_CHUNK, hd), BF16),
            pltpu.VMEM((n_units, GDN_CHUNK, hd), F32),
            pltpu.VMEM((n_units, 1, hd), F32),
        ],
        compiler_params=_params(("parallel", "parallel", "arbitrary")),
        name="gdn",
    )(proj_g, proj_g, proj_g, proj_g, proj_ab, conv_w, conv_w, conv_w,
      on_a_lanes(a_log), on_a_lanes(dt_bias), norm_w.reshape(1, hd))


def _moba_kernel(q_ref, k_ref, v_ref, slope_ref, o_ref, kaug_ref, qaug_ref, kmean_ref, kparts_ref, sel_ref, sa_ref, sb_ref, acc_ref, vt_ref,
                 ta_ref, tb_ref):
    blk = MOBA_BLOCK
    hd = HEAD_DIM
    grp = MOBA_GROUP
    hps = q_ref.shape[1] // hd
    heads = range(hps)
    nb = k_ref.shape[0] // blk
    i = pl.program_id(2)
    lanes = lambda hh: slice(hh * hd, (hh + 1) * hd)
    slope2 = [slope_ref[hh] for hh in heads]

    @pl.when(i == 0)
    def _():
        pos = lax.broadcasted_iota(jnp.int32, (blk, hd), 0).astype(F32)
        lane = lax.broadcasted_iota(jnp.int32, (blk, hd), 1)
        one = jnp.ones((blk, hd), BF16)
        zero = jnp.zeros((blk, hd), BF16)
        ones_row = (lax.broadcasted_iota(jnp.int32, (MOBA_VROWS - hd, blk), 0) == 0).astype(BF16)
        k_extra = []
        for hh in heads:
            hi, mid, lo = _split3(slope2[hh][:, :hd] * pos)
            pieces = jnp.where(lane == 0, hi, jnp.where(lane == 1, mid, jnp.where(lane == 2, lo, zero)))
            k_extra.append(jnp.where((lane >= 3) & (lane < 6), one, pieces))
            npieces = jnp.where(lane == 3, -hi, jnp.where(lane == 4, -mid, jnp.where(lane == 5, -lo, zero)))
            qaug_ref[hh, :, hd:] = jnp.where(lane < 3, one, npieces)

        def block_step(j, carry):
            r0 = pl.multiple_of(j * blk, blk)
            for hh in heads:
                kb = k_ref[pl.ds(r0, blk), lanes(hh)]
                kaug_ref[hh, pl.ds(r0, blk), 0:hd] = kb
                kaug_ref[hh, pl.ds(r0, blk), hd:] = k_extra[hh]
                kmean_ref[hh, pl.ds(j, 1), :] = jnp.mean(kb.astype(F32), axis=0, keepdims=True)
                vt_ref[hh, j, 0:hd, :] = v_ref[pl.ds(r0, blk), lanes(hh)].astype(F32).T.astype(BF16)
                vt_ref[hh, j, hd:, :] = ones_row
            return carry
        lax.fori_loop(0, nb, block_step, 0)
        for hh in heads:
            for part, piece in enumerate(_split3(kmean_ref[hh])):
                kparts_ref[hh, part * nb:(part + 1) * nb, :] = piece

    blk_id = lax.broadcasted_iota(jnp.int32, (nb, blk), 0)
    key_pos = lax.broadcasted_iota(jnp.int32, (blk, blk), 0)
    qry_pos = lax.broadcasted_iota(jnp.int32, (blk, blk), 1)
    r0 = pl.multiple_of(i * blk, blk)
    def score_group(g, dst_ref, top_ref):
        for b in range(grp):
            c0 = pl.multiple_of((g * grp + b) * blk, blk)
            for hh in heads:
                s2 = _dot_nt(kaug_ref[hh, pl.ds(c0, blk), :], qa[hh])
                dst_ref[hh * grp + b] = s2
                top_ref[hh * grp + b] = jnp.max(s2, axis=0, keepdims=True)

    qa, gates, own = [], [], []
    for hh in heads:
        q = q_ref[:, lanes(hh)]
        qaug_ref[hh, :, 0:hd] = q
        qa.append(qaug_ref[hh])
        g3 = _dot_nt(kparts_ref[hh], q)
        gates.append(g3[0:nb] + g3[nb:2 * nb] + g3[2 * nb:3 * nb])
        own.append(_dot_nt(kaug_ref[hh, pl.ds(r0, blk), :], qa[hh]))
    score_group(0, sa_ref, ta_ref)
    score_group(1, sb_ref, tb_ref)

    start = []
    for hh in heads:
        gate = jnp.where(blk_id < i, gates[hh], NEG_INF)
        sel = jnp.zeros((nb, blk), F32)
        for _ in range(MOBA_TOPK):
            best = jnp.max(gate, axis=0, keepdims=True)
            hit = (gate == best) & (best > NEG_INF)
            first = jnp.min(jnp.where(hit, blk_id, nb), axis=0, keepdims=True)
            pick = blk_id == first
            sel = jnp.where(pick, 1.0, sel)
            gate = jnp.where(pick, NEG_INF, gate)
        sel_ref[hh] = sel
        s = jnp.where(key_pos <= qry_pos, own[hh], NEG_INF)
        m0 = jnp.max(s, axis=0, keepdims=True)
        p = jnp.exp2(s - m0)
        acc_ref[hh] = _dot(vt_ref[hh, i], p.astype(BF16))
        start.append(m0)

    def group_max(g, top_ref, carry):
        subs = [[] for _ in heads]
        m_new = [carry[hh] for hh in heads]
        for b in range(grp):
            j = g * grp + b
            for hh in heads:
                chosen = sel_ref[hh, pl.ds(j, 1), :] > 0.0
                off = slope2[hh] * ((i - j) * blk).astype(F32)
                top = top_ref[hh * grp + b] - off
                m_new[hh] = jnp.maximum(m_new[hh], jnp.where(chosen, top, NEG_INF))
                subs[hh].append((chosen, off))
        return m_new, subs

    last_group = nb // grp - 1
    n_groups = (i + grp - 1) // grp
    n_pairs = n_groups // 2

    def attend_group(g, src_ref, top_ref, carry, refill):
        m_new, subs = group_max(g, top_ref, carry)
        alpha = [jnp.exp2(carry[hh] - m_new[hh]) for hh in heads]
        nxt = jnp.minimum(g + 2, last_group)
        pv = [None for _ in heads]
        for b in range(grp):
            c0 = pl.multiple_of((nxt * grp + b) * blk, blk)
            for hh in heads:
                chosen, off = subs[hh][b]
                p2 = jnp.exp2((src_ref[hh * grp + b] - jnp.where(chosen, m_new[hh] + off, jnp.inf)).astype(BF16))
                part = _dot(vt_ref[hh, g * grp + b], p2)
                pv[hh] = part if pv[hh] is None else pv[hh] + part
                if refill:
                    s2 = _dot_nt(kaug_ref[hh, pl.ds(c0, blk), :], qa[hh])
                    src_ref[hh * grp + b] = s2
                    top_ref[hh * grp + b] = jnp.max(s2, axis=0, keepdims=True)
        for hh in heads:
            acc_ref[hh] = acc_ref[hh] * alpha[hh] + pv[hh]
        return tuple(m_new)

    def past_pair(gp, carry, refill_odd=True):
        carry = attend_group(2 * gp, sa_ref, ta_ref, carry, True)
        return attend_group(2 * gp + 1, sb_ref, tb_ref, carry, refill_odd)

    final = lax.fori_loop(0, jnp.maximum(n_pairs - 1, 0), past_pair, tuple(start))
    final = lax.cond(n_pairs > 0, lambda c: past_pair(n_pairs - 1, c, False), lambda c: c, final)

    @pl.when(n_groups % 2 == 1)
    def _():
        attend_group(n_groups - 1, sa_ref, ta_ref, final, False)

    for hh in heads:
        o_ref[:, lanes(hh)] = (acc_ref[hh, 0:hd, :] / acc_ref[hh, hd:hd + 1, :]).T.astype(o_ref.dtype)


def _moba(proj_m, slopes, batch, seq, *, hps):
    m = proj_m.shape[0]
    blk, hd = MOBA_BLOCK, HEAD_DIM
    wide = hps * hd
    ng = MOBA_HEADS // hps
    nb = seq // blk
    assert nb % (2 * MOBA_GROUP) == 0
    slope_b = jnp.broadcast_to((slopes.astype(F32) * LOG2E)[:, None, None], (MOBA_HEADS, 1, blk))
    whole_seq = lambda part: pl.BlockSpec((seq, wide), lambda b, hg, i: (b, part * ng + hg),
                                          pipeline_mode=pl.Buffered(1))
    return pl.pallas_call(
        _moba_kernel,
        grid=(batch, ng, nb),
        in_specs=[
            pl.BlockSpec((blk, wide), lambda b, hg, i: (b * nb + i, hg)),
            whole_seq(1),
            whole_seq(2),
            pl.BlockSpec((hps, 1, blk), lambda b, hg, i: (hg, 0, 0)),
        ],
        out_specs=pl.BlockSpec((blk, wide), lambda b, hg, i: (b * nb + i, hg)),
        out_shape=jax.ShapeDtypeStruct((m, MOBA_HEADS * hd), BF16),
        scratch_shapes=[
            pltpu.VMEM((hps, seq, 2 * hd), BF16),
            pltpu.VMEM((hps, blk, 2 * hd), BF16),
            pltpu.VMEM((hps, nb, hd), F32),
            pltpu.VMEM((hps, 3 * nb, hd), BF16),
            pltpu.VMEM((hps, nb, blk), F32),
            pltpu.VMEM((hps * MOBA_GROUP, blk, blk), F32),
            pltpu.VMEM((hps * MOBA_GROUP, blk, blk), F32),
            pltpu.VMEM((hps, MOBA_VROWS, blk), F32),
            pltpu.VMEM((hps, nb, MOBA_VROWS, blk), BF16),
            pltpu.VMEM((hps * MOBA_GROUP, 1, blk), F32),
            pltpu.VMEM((hps * MOBA_GROUP, 1, blk), F32),
        ],
        compiler_params=_params(("parallel", "parallel", "arbitrary")),
        name="moba",
    )(proj_m, proj_m, proj_m, slope_b)


def _mem_attn_kernel(x_ref, gpre_ref, wq_ref, kv_ref, wo_ref, gpost_ref, o_ref):
    hd = HEAD_DIM
    width = MEM_HEADS * hd
    x = x_ref[...]
    hn = _rms(x, gpre_ref[...]).astype(BF16)
    q = (_dot(hn, wq_ref[...]) * (hd ** -0.5)).astype(BF16)
    kv = kv_ref[...]
    heads = []
    for hh in range(MEM_HEADS):
        s = _dot_nt(q[:, hh * hd:(hh + 1) * hd], kv[:, hh * hd:(hh + 1) * hd])
        p = jnp.exp(s - jnp.max(s, axis=-1, keepdims=True))
        p = p / jnp.sum(p, axis=-1, keepdims=True)
        heads.append(_dot(p.astype(BF16), kv[:, width + hh * hd:width + (hh + 1) * hd]))
    o = jnp.concatenate(heads, axis=-1).astype(BF16)
    o_ref[...] = x + _rms(_dot(o, wo_ref[...]), gpost_ref[...])


def _mem_attn(x2d, g_pre, wq_bf16, kv_bf16, wo_bf16, g_post, batch, seq, *, tm):
    m, d = x2d.shape
    nt = seq // tm
    mem_len, kvw = kv_bf16.shape[1], kv_bf16.shape[2]
    width = wq_bf16.shape[1]
    return pl.pallas_call(
        _mem_attn_kernel,
        grid=(batch, nt),
        in_specs=[
            pl.BlockSpec((tm, d), lambda b, t: (b * nt + t, 0)),
            pl.BlockSpec((1, d), lambda b, t: (0, 0)),
            pl.BlockSpec((d, width), lambda b, t: (0, 0)),
            pl.BlockSpec((None, mem_len, kvw), lambda b, t: (b, 0, 0)),
            pl.BlockSpec((width, d), lambda b, t: (0, 0)),
            pl.BlockSpec((1, d), lambda b, t: (0, 0)),
        ],
        out_specs=pl.BlockSpec((tm, d), lambda b, t: (b * nt + t, 0)),
        out_shape=jax.ShapeDtypeStruct((m, d), F32),
        compiler_params=_params(("parallel", "parallel")),
        name="mem_attn",
    )(x2d, g_pre.reshape(1, d), wq_bf16, kv_bf16, wo_bf16, g_post.reshape(1, d))


def kernel(x, mem, pre_mix_norm, w_in, conv_w, a_log, dt_bias, gdn_norm_w, w_out, post_mix_norm,
           pre_mem_norm, mem_kv_norm, w_mq, w_mk, w_mv, w_mo, post_mem_norm,
           pre_mlp_norm, w_up, w_down, post_mlp_norm):
    batch, seq, d = x.shape
    mem_len = mem.shape[1]
    gw = GDN_HEADS * HEAD_DIM
    mw = MOBA_HEADS * HEAD_DIM
    assert seq % MOBA_BLOCK == 0 and seq % 512 == 0
    x2d = x.reshape(batch * seq, d)
    mem2d = mem.reshape(batch * mem_len, d)
    slopes = jnp.exp2(-8.0 * jnp.arange(1, MOBA_HEADS + 1, dtype=F32) / MOBA_HEADS)
    ones = lambda n: jnp.ones((n,), F32)

    for l in range(w_in.shape[0]):
        win = w_in[l]
        w_g = win[:, :4 * gw].astype(BF16)
        w_ab = jnp.pad(win[:, 4 * gw:4 * gw + 2 * GDN_HEADS], ((0, 0), (0, HEAD_DIM - 2 * GDN_HEADS))).astype(BF16)
        w_m = win[:, 4 * gw + 2 * GDN_HEADS:].astype(BF16)
        q_scale = jnp.concatenate([jnp.full((mw,), LOG2E * HEAD_DIM ** -0.5, F32), ones(2 * mw)])

        proj_g, proj_ab = _norm_matmul(x2d, pre_mix_norm[l], w_g, ones(4 * gw), F32, tm=1024, tn=1024,
                                       side_w=w_ab, name="in_proj_gdn")
        proj_m = _norm_matmul(x2d, pre_mix_norm[l], w_m, q_scale, BF16, tm=1024, tn=1024, name="in_proj_moba")

        y_gdn = _gdn(proj_g, proj_ab, conv_w[l], a_log[l], dt_bias[l], gdn_norm_w[l], batch, seq, tt=256, hps=8)
        y_moba = _moba(proj_m, slopes, batch, seq, hps=2)
        x2d = _matmul_norm_res([y_gdn, y_moba], w_out[l].astype(BF16), post_mix_norm[l], x2d,
                               tm=512, tk=gw + mw, name="out_proj")

        w_kv = jnp.concatenate([w_mk[l], w_mv[l]], axis=1).astype(BF16)
        kv = _norm_matmul(mem2d, mem_kv_norm[l], w_kv, ones(w_kv.shape[1]), BF16, tm=512, tn=1024, name="mem_kv")
        x2d = _mem_attn(x2d, pre_mem_norm[l], w_mq[l].astype(BF16), kv.reshape(batch, mem_len, -1),
                        w_mo[l].astype(BF16), post_mem_norm[l], batch, seq, tm=512)

        hid = _norm_matmul(x2d, pre_mlp_norm[l], w_up[l].astype(BF16), ones(w_up.shape[2]), BF16,
                           tm=1024, tn=1024, relu2=True, name="mlp_up")
        x2d = _matmul_norm_res([hid], w_down[l].astype(BF16), post_mlp_norm[l], x2d, tm=512, tk=2048, name="mlp_down")
    return x2d.reshape(batch, seq, d)
```

```python
import functools
import math

import jax
import jax.numpy as jnp
from jax import lax
from jax.experimental import pallas as pl
from jax.experimental.pallas import tpu as pltpu

F32 = jnp.float32
BF16 = jnp.bfloat16

HEAD_DIM = 128
GDN_HEADS = 8
MOBA_HEADS = 8
CONV_WIDTH = 4
GDN_CHUNK = 64
GDN_SUB = 16
MOBA_BLOCK = 256
MOBA_TOPK = 3
MOBA_GROUP = 4
MOBA_VROWS = HEAD_DIM + 16
MEM_HEADS = 4
NORM_EPS = 1e-6
NORM_CHUNKS = 4
LOG2E = math.log2(math.e)

V7X_VMEM_BYTES = 64 * 1024 * 1024
VMEM_LIMIT = V7X_VMEM_BYTES - 8 * 1024 * 1024

NEG_INF = float("-inf")


def _params(semantics):
    return pltpu.CompilerParams(dimension_semantics=semantics, vmem_limit_bytes=VMEM_LIMIT)


def _dot(a, b):
    return jnp.dot(a, b, preferred_element_type=F32)


def _dot_nt(a, b):
    return lax.dot_general(a, b, (((1,), (1,)), ((), ())), preferred_element_type=F32)


def _dot_tn(a, b):
    return lax.dot_general(a, b, (((0,), (0,)), ((), ())), preferred_element_type=F32)


def _bdot(a, b):
    return _dot(a.astype(BF16), b.astype(BF16))


def _split3(x):
    hi = x.astype(BF16)
    rest = x - hi.astype(F32)
    mid = rest.astype(BF16)
    lo = (rest - mid.astype(F32)).astype(BF16)
    return hi, mid, lo


def _rms(y, gain):
    return y * lax.rsqrt(jnp.mean(y * y, axis=-1, keepdims=True) + NORM_EPS) * gain


def _norm_matmul_kernel(*refs, relu2, side):
    if side:
        x_ref, g_ref, w_ref, cs_ref, ws_ref, o_ref, os_ref, hn_ref = refs
    else:
        x_ref, g_ref, w_ref, cs_ref, o_ref, hn_ref = refs

    def project(hn, rows):
        y = _dot(hn, w_ref[...]) * cs_ref[...]
        if relu2:
            y = jnp.square(jnp.maximum(y, 0.0))
        o_ref[rows, :] = y.astype(o_ref.dtype)

    @pl.when(pl.program_id(1) == 0)
    def _():
        step = x_ref.shape[0] // NORM_CHUNKS
        for r in range(NORM_CHUNKS):
            rows = slice(r * step, (r + 1) * step)
            hn = _rms(x_ref[rows, :], g_ref[...]).astype(BF16)
            hn_ref[rows, :] = hn
            if side:
                os_ref[rows, :] = _dot(hn, ws_ref[...])
            project(hn, rows)

    @pl.when(pl.program_id(1) != 0)
    def _():
        project(hn_ref[...], slice(None))


def _norm_matmul(x2d, gain, w_bf16, col_scale, out_dtype, *, tm, tn, relu2=False, side_w=None, name):
    m, d = x2d.shape
    n = w_bf16.shape[1]
    tm, tn = min(tm, m), min(tn, n)
    side = side_w is not None
    in_specs = [
        pl.BlockSpec((tm, d), lambda i, j: (i, 0)),
        pl.BlockSpec((1, d), lambda i, j: (0, 0)),
        pl.BlockSpec((d, tn), lambda i, j: (0, j)),
        pl.BlockSpec((1, tn), lambda i, j: (0, j)),
    ]
    out_specs = pl.BlockSpec((tm, tn), lambda i, j: (i, j))
    out_shape = jax.ShapeDtypeStruct((m, n), out_dtype)
    args = [x2d, gain.reshape(1, d), w_bf16, col_scale.reshape(1, n)]
    if side:
        ns = side_w.shape[1]
        in_specs.append(pl.BlockSpec((d, ns), lambda i, j: (0, 0)))
        out_specs = [out_specs, pl.BlockSpec((tm, ns), lambda i, j: (i, 0))]
        out_shape = [out_shape, jax.ShapeDtypeStruct((m, ns), F32)]
        args.append(side_w)
    return pl.pallas_call(
        functools.partial(_norm_matmul_kernel, relu2=relu2, side=side),
        grid=(m // tm, n // tn),
        in_specs=in_specs,
        out_specs=out_specs,
        out_shape=out_shape,
        scratch_shapes=[pltpu.VMEM((tm, d), BF16)],
        compiler_params=_params(("parallel", "arbitrary")),
        name=name,
    )(*args)


def _matmul_norm_res_kernel(*refs, n_in, n_k):
    a_refs = refs[:n_in]
    w_ref, g_ref, r_ref, o_ref, acc_ref = refs[n_in:]
    k = pl.program_id(1)
    last = n_k - 1

    def product(rows):
        off = 0
        acc = None
        for a_ref in a_refs:
            width = a_ref.shape[1]
            part = _dot(a_ref[rows, :], w_ref[off:off + width, :])
            acc = part if acc is None else acc + part
            off += width
        return acc

    def finish(carried):
        step = o_ref.shape[0] // NORM_CHUNKS
        for r in range(NORM_CHUNKS):
            rows = slice(r * step, (r + 1) * step)
            acc = product(rows) + acc_ref[rows, :] if carried else product(rows)
            o_ref[rows, :] = r_ref[rows, :] + _rms(acc, g_ref[...])

    if n_k == 1:
        finish(False)
        return

    @pl.when(k == 0)
    def _():
        acc_ref[...] = product(slice(None))

    @pl.when((k != 0) & (k != last))
    def _():
        acc_ref[...] += product(slice(None))

    @pl.when(k == last)
    def _():
        finish(True)


def _matmul_norm_res(a_list, w_bf16, gain, resid, *, tm, tk, name):
    m, d = resid.shape
    ktot = w_bf16.shape[0]
    n_in = len(a_list)
    tm = min(tm, m)
    if n_in > 1:
        tk = ktot
        a_specs = [pl.BlockSpec((tm, a.shape[1]), lambda i, k: (i, 0)) for a in a_list]
    else:
        tk = min(tk, ktot)
        a_specs = [pl.BlockSpec((tm, tk), lambda i, k: (i, k))]
    return pl.pallas_call(
        functools.partial(_matmul_norm_res_kernel, n_in=n_in, n_k=ktot // tk),
        grid=(m // tm, ktot // tk),
        in_specs=a_specs + [
            pl.BlockSpec((tk, d), lambda i, k: (k, 0)),
            pl.BlockSpec((1, d), lambda i, k: (0, 0)),
            pl.BlockSpec((tm, d), lambda i, k: (i, 0)),
        ],
        out_specs=pl.BlockSpec((tm, d), lambda i, k: (i, 0)),
        out_shape=jax.ShapeDtypeStruct((m, d), F32),
        scratch_shapes=[pltpu.VMEM((tm, d), F32)],
        compiler_params=_params(("parallel", "arbitrary")),
        name=name,
    )(*a_list, w_bf16, gain.reshape(1, d), resid)


def _gdn_kernel(q_ref, k_ref, v_ref, z_ref, ab_ref, cwq_ref, cwk_ref, cwv_ref, alog_ref, dtb_ref, nw_ref,
                o_ref, state_ref, tail_ref, pad_ref, qs_ref, ks_ref, vs_ref, gs_ref, bs_ref, os_ref,
                w2_ref, r_ref, qp_ref, op_ref, gl_ref):
    tt = q_ref.shape[0]
    c = GDN_CHUNK
    hd = HEAD_DIM
    hps = q_ref.shape[1] // hd
    n_chunks = tt // c
    head0 = pl.program_id(1) * hps
    lanes = lambda hh: slice(hh * hd, (hh + 1) * hd)

    @pl.when(pl.program_id(2) == 0)
    def _():
        state_ref[...] = jnp.zeros_like(state_ref)
        tail_ref[...] = jnp.zeros_like(tail_ref)

    def conv_silu(x_ref, slot, cw_ref):
        pad_ref[0:8, :] = tail_ref[slot]
        pad_ref[8:, :] = x_ref[...]
        tail_ref[slot] = x_ref[tt - 8:tt, :]
        y = cw_ref[0:1, :] * pad_ref[pl.ds(8 - (CONV_WIDTH - 1), tt), :]
        for j in range(1, CONV_WIDTH):
            y = y + cw_ref[j:j + 1, :] * pad_ref[pl.ds(8 - (CONV_WIDTH - 1) + j, tt), :]
        return y * jax.nn.sigmoid(y)

    def l2n(y):
        return y * lax.rsqrt(jnp.sum(y * y, axis=-1, keepdims=True) + NORM_EPS)

    qc = conv_silu(q_ref, 0, cwq_ref)
    kc = conv_silu(k_ref, 1, cwk_ref)
    vs_ref[...] = conv_silu(v_ref, 2, cwv_ref)
    ab = ab_ref[...]
    lane = lax.broadcasted_iota(jnp.int32, ab.shape, 1)
    xa = ab + dtb_ref[...]
    softplus = jnp.maximum(xa, 0.0) + jnp.log(1.0 + jnp.exp(-jnp.abs(xa)))
    g_all = -(jnp.exp(alog_ref[...]) * softplus)
    beta_all = jax.nn.sigmoid(ab)
    for hh in range(hps):
        qs_ref[:, lanes(hh)] = l2n(qc[:, lanes(hh)]) * (HEAD_DIM ** -0.5)
        ks_ref[:, lanes(hh)] = l2n(kc[:, lanes(hh)])
        g_col = jnp.sum(jnp.where(lane == head0 + hh, g_all, 0.0), axis=-1, keepdims=True)
        b_col = jnp.sum(jnp.where(lane == head0 + hh + GDN_HEADS, beta_all, 0.0), axis=-1, keepdims=True)
        gs_ref[:, lanes(hh)] = jnp.broadcast_to(g_col, (tt, hd))
        bs_ref[:, lanes(hh)] = jnp.broadcast_to(b_col, (tt, hd))

    row = lax.broadcasted_iota(jnp.int32, (c, c), 0)
    col = lax.broadcasted_iota(jnp.int32, (c, c), 1)
    causal = row >= col
    strict = row > col
    same_sub = (row // GDN_SUB) == (col // GDN_SUB)
    ltri3 = jnp.concatenate([causal.astype(BF16)] * 3, axis=1)

    units = [(n, hh) for n in range(n_chunks) for hh in range(hps)]
    each = lambda fn, *cols: [fn(*args) for args in zip(*cols)]
    rows = lambda ref: [ref[n * c:(n + 1) * c, lanes(hh)] for n, hh in units]
    q, k, v, g, beta = rows(qs_ref), rows(ks_ref), rows(vs_ref), rows(gs_ref), rows(bs_ref)
    pieces = each(_split3, g)
    gc = each(lambda p: _dot(ltri3, jnp.concatenate(p, axis=0)), pieces)
    dexp = each(lambda p: _dot(ltri3, jnp.concatenate([jnp.where(strict, x[:, :c], 0.0) for x in p], axis=0)), pieces)
    decay = each(lambda d: jnp.where(causal, jnp.exp(d), 0.0), dexp)
    kb = each(lambda a, b: a * b, k, beta)
    kbf = each(lambda a: a.astype(BF16), k)
    mm = each(lambda a, b, d: jnp.where(strict, _dot_nt(a.astype(BF16), b) * d, 0.0), kb, kbf, decay)
    a_qk = each(lambda a, b, d: (_dot_nt(a.astype(BF16), b) * d).astype(BF16), q, kbf, decay)
    x = each(lambda m_: jnp.where(same_sub, -m_, 0.0), mm)
    lo = each(lambda m_: jnp.where(same_sub, 0.0, m_), mm)
    x2 = each(_bdot, x, x)
    x4 = each(_bdot, x2, x2)
    a1 = each(lambda a, b: a + b + _bdot(a, b), x, x2)
    x8 = each(_bdot, x4, x4)
    a2 = each(lambda a, b: a + b + _bdot(a, b), a1, x4)
    dm = each(lambda a, b: a + b + _bdot(a, b), a2, x8)
    y = each(lambda d, l_: -(l_ + _bdot(d, l_)), dm, lo)
    y2 = each(_bdot, y, y)
    qm = each(lambda a, b: a + b + _bdot(a, b), y, y2)
    tm = each(lambda a, b: a + b + _bdot(a, b), qm, dm)
    eg = each(jnp.exp, gc)
    uw = each(lambda v_, b_, kb_, e_: jnp.concatenate([v_ * b_, kb_ * e_], axis=1), v, beta, kb, eg)
    uw = each(lambda t_, u_: (u_ + _bdot(t_, u_)).astype(BF16), tm, uw)
    au = each(_dot, a_qk, uw)
    kt = each(lambda k_, g_: (k_ * jnp.exp(g_[c - 1:c, :] - g_)).astype(BF16), k, gc)
    kw = each(_dot_tn, kt, uw)
    for u in range(len(units)):
        op_ref[u] = au[u][:, :hd]
        qp_ref[u] = (q[u] * eg[u] - au[u][:, hd:]).astype(BF16)
        r_ref[u] = kw[u][:, :hd]
        w2_ref[u] = kw[u][:, hd:].astype(BF16)
        gl_ref[u] = jnp.exp(gc[u][c - 1:c, :])

    s = [state_ref[hh] for hh in range(hps)]
    for u, (n, hh) in enumerate(units):
        sb = s[hh].astype(BF16)
        os_ref[n * c:(n + 1) * c, lanes(hh)] = op_ref[u] + _dot(qp_ref[u], sb)
        s[hh] = s[hh] * gl_ref[u] + r_ref[u] - _dot(w2_ref[u], sb)
    for hh in range(hps):
        state_ref[hh] = s[hh]
        z = z_ref[:, lanes(hh)]
        o_ref[:, lanes(hh)] = (_rms(os_ref[:, lanes(hh)], nw_ref[...]) * (z * jax.nn.sigmoid(z))).astype(o_ref.dtype)


def _gdn(proj_g, proj_ab, conv_w, a_log, dt_bias, norm_w, batch, seq, *, tt, hps):
    m = proj_g.shape[0]
    nt = seq // tt
    hd = HEAD_DIM
    wide = hps * hd
    ng = GDN_HEADS // hps
    n_units = (tt // GDN_CHUNK) * hps
    row = lambda b, hg, t: b * nt + t
    on_a_lanes = lambda p: jnp.pad(p.astype(F32), (0, hd - GDN_HEADS)).reshape(1, hd)
    tile = lambda part: pl.BlockSpec((tt, wide), lambda b, hg, t: (row(b, hg, t), part * ng + hg))
    cw = lambda part: pl.BlockSpec((CONV_WIDTH, wide), lambda b, hg, t: (0, part * ng + hg))
    per_head = pl.BlockSpec((1, hd), lambda b, hg, t: (0, 0))
    return pl.pallas_call(
        _gdn_kernel,
        grid=(batch, ng, nt),
        in_specs=[
            tile(0), tile(1), tile(2), tile(3),
            pl.BlockSpec((tt, hd), lambda b, hg, t: (row(b, hg, t), 0)),
            cw(0), cw(1), cw(2),
            per_head, per_head,
            pl.BlockSpec((1, hd), lambda b, hg, t: (0, 0)),
        ],
        out_specs=pl.BlockSpec((tt, wide), lambda b, hg, t: (row(b, hg, t), hg)),
        out_shape=jax.ShapeDtypeStruct((m, GDN_HEADS * hd), BF16),
        scratch_shapes=[
            pltpu.VMEM((hps, hd, hd), F32),
            pltpu.VMEM((3, 8, wide), F32),
            pltpu.VMEM((tt + 8, wide), F32),
            pltpu.VMEM((tt, wide), F32),
            pltpu.VMEM((tt, wide), F32),
            pltpu.VMEM((tt, wide), F32),
            pltpu.VMEM((tt, wide), F32),
            pltpu.VMEM((tt, wide), F32),
            pltpu.VMEM((tt, wide), F32),
            pltpu.VMEM((n_units, hd, hd), BF16),
            pltpu.VMEM((n_units, hd, hd), F32),
            pltpu.VMEM((n_units, GDN_CHUNK, hd), BF16),
            pltpu.VMEM((n_units, GDN_CHUNK, hd), F32),
            pltpu.VMEM((n_units, 1, hd), F32),
        ],
        compiler_params=_params(("parallel", "parallel", "arbitrary")),
        name="gdn",
    )(proj_g, proj_g, proj_g, proj_g, proj_ab, conv_w, conv_w, conv_w,
      on_a_lanes(a_log), on_a_lanes(dt_bias), norm_w.reshape(1, hd))


def _moba_kernel(q_ref, k_ref, v_ref, slope_ref, o_ref, kaug_ref, qaug_ref, kmean_ref, kparts_ref, sel_ref, sa_ref, sb_ref, acc_ref, vt_ref,
                 ta_ref, tb_ref):
    blk = MOBA_BLOCK
    hd = HEAD_DIM
    grp = MOBA_GROUP
    hps = q_ref.shape[1] // hd
    heads = range(hps)
    nb = k_ref.shape[0] // blk
    i = pl.program_id(2)
    lanes = lambda hh: slice(hh * hd, (hh + 1) * hd)
    slope2 = [slope_ref[hh] for hh in heads]

    @pl.when(i == 0)
    def _():
        pos = lax.broadcasted_iota(jnp.int32, (blk, hd), 0).astype(F32)
        lane = lax.broadcasted_iota(jnp.int32, (blk, hd), 1)
        one = jnp.ones((blk, hd), BF16)
        zero = jnp.zeros((blk, hd), BF16)
        ones_row = (lax.broadcasted_iota(jnp.int32, (MOBA_VROWS - hd, blk), 0) == 0).astype(BF16)
        k_extra = []
        for hh in heads:
            hi, mid, lo = _split3(slope2[hh][:, :hd] * pos)
            pieces = jnp.where(lane == 0, hi, jnp.where(lane == 1, mid, jnp.where(lane == 2, lo, zero)))
            k_extra.append(jnp.where((lane >= 3) & (lane < 6), one, pieces))
            npieces = jnp.where(lane == 3, -hi, jnp.where(lane == 4, -mid, jnp.where(lane == 5, -lo, zero)))
            qaug_ref[hh, :, hd:] = jnp.where(lane < 3, one, npieces)

        def block_step(j, carry):
            r0 = pl.multiple_of(j * blk, blk)
            for hh in heads:
                kb = k_ref[pl.ds(r0, blk), lanes(hh)]
                kaug_ref[hh, pl.ds(r0, blk), 0:hd] = kb
                kaug_ref[hh, pl.ds(r0, blk), hd:] = k_extra[hh]
                kmean_ref[hh, pl.ds(j, 1), :] = jnp.mean(kb.astype(F32), axis=0, keepdims=True)
                vt_ref[hh, j, 0:hd, :] = v_ref[pl.ds(r0, blk), lanes(hh)].astype(F32).T.astype(BF16)
                vt_ref[hh, j, hd:, :] = ones_row
            return carry
        lax.fori_loop(0, nb, block_step, 0)
        for hh in heads:
            for part, piece in enumerate(_split3(kmean_ref[hh])):
                kparts_ref[hh, part * nb:(part + 1) * nb, :] = piece

    blk_id = lax.broadcasted_iota(jnp.int32, (nb, blk), 0)
    key_pos = lax.broadcasted_iota(jnp.int32, (blk, blk), 0)
    qry_pos = lax.broadcasted_iota(jnp.int32, (blk, blk), 1)
    r0 = pl.multiple_of(i * blk, blk)
    def score_group(g, dst_ref, top_ref):
        for b in range(grp):
            c0 = pl.multiple_of((g * grp + b) * blk, blk)
            for hh in heads:
                s2 = _dot_nt(kaug_ref[hh, pl.ds(c0, blk), :], qa[hh])
                dst_ref[hh * grp + b] = s2
                top_ref[hh * grp + b] = jnp.max(s2, axis=0, keepdims=True)

    qa, gates, own = [], [], []
    for hh in heads:
        q = q_ref[:, lanes(hh)]
        qaug_ref[hh, :, 0:hd] = q
        qa.append(qaug_ref[hh])
        g3 = _dot_nt(kparts_ref[hh], q)
        gates.append(g3[0:nb] + g3[nb:2 * nb] + g3[2 * nb:3 * nb])
        own.append(_dot_nt(kaug_ref[hh, pl.ds(r0, blk), :], qa[hh]))
    score_group(0, sa_ref, ta_ref)
    score_group(1, sb_ref, tb_ref)

    start = []
    for hh in heads:
        gate = jnp.where(blk_id < i, gates[hh], NEG_INF)
        sel = jnp.zeros((nb, blk), F32)
        for _ in range(MOBA_TOPK):
            best = jnp.max(gate, axis=0, keepdims=True)
            hit = (gate == best) & (best > NEG_INF)
            first = jnp.min(jnp.where(hit, blk_id, nb), axis=0, keepdims=True)
            pick = blk_id == first
            sel = jnp.where(pick, 1.0, sel)
            gate = jnp.where(pick, NEG_INF, gate)
        sel_ref[hh] = sel
        s = jnp.where(key_pos <= qry_pos, own[hh], NEG_INF)
        m0 = jnp.max(s, axis=0, keepdims=True)
        p = jnp.exp2(s - m0)
        acc_ref[hh] = _dot(vt_ref[hh, i], p.astype(BF16))
        start.append(m0)

    def group_max(g, top_ref, carry):
        subs = [[] for _ in heads]
        m_new = [carry[hh] for hh in heads]
        for b in range(grp):
            j = g * grp + b
            for hh in heads:
                chosen = sel_ref[hh, pl.ds(j, 1), :] > 0.0
                off = slope2[hh] * ((i - j) * blk).astype(F32)
                top = top_ref[hh * grp + b] - off
                m_new[hh] = jnp.maximum(m_new[hh], jnp.where(chosen, top, NEG_INF))
                subs[hh].append((chosen, off))
        return m_new, subs

    last_group = nb // grp - 1
    n_groups = (i + grp - 1) // grp
    n_pairs = n_groups // 2

    def attend_group(g, src_ref, top_ref, carry, refill):
        m_new, subs = group_max(g, top_ref, carry)
        alpha = [jnp.exp2(carry[hh] - m_new[hh]) for hh in heads]
        nxt = jnp.minimum(g + 2, last_group)
        pv = [None for _ in heads]
        for b in range(grp):
            c0 = pl.multiple_of((nxt * grp + b) * blk, blk)
            for hh in heads:
                chosen, off = subs[hh][b]
                p2 = jnp.exp2((src_ref[hh * grp + b] - jnp.where(chosen, m_new[hh] + off, jnp.inf)).astype(BF16))
                part = _dot(vt_ref[hh, g * grp + b], p2)
                pv[hh] = part if pv[hh] is None else pv[hh] + part
                if refill:
                    s2 = _dot_nt(kaug_ref[hh, pl.ds(c0, blk), :], qa[hh])
                    src_ref[hh * grp + b] = s2
                    top_ref[hh * grp + b] = jnp.max(s2, axis=0, keepdims=True)
        for hh in heads:
            acc_ref[hh] = acc_ref[hh] * alpha[hh] + pv[hh]
        return tuple(m_new)

    def past_pair(gp, carry, refill_odd=True):
        carry = attend_group(2 * gp, sa_ref, ta_ref, carry, True)
        return attend_group(2 * gp + 1, sb_ref, tb_ref, carry, refill_odd)

    final = lax.fori_loop(0, jnp.maximum(n_pairs - 1, 0), past_pair, tuple(start))
    final = lax.cond(n_pairs > 0, lambda c: past_pair(n_pairs - 1, c, False), lambda c: c, final)

    @pl.when(n_groups % 2 == 1)
    def _():
        attend_group(n_groups - 1, sa_ref, ta_ref, final, False)

    for hh in heads:
        o_ref[:, lanes(hh)] = (acc_ref[hh, 0:hd, :] / acc_ref[hh, hd:hd + 1, :]).T.astype(o_ref.dtype)


def _moba(proj_m, slopes, batch, seq, *, hps):
    m = proj_m.shape[0]
    blk, hd = MOBA_BLOCK, HEAD_DIM
    wide = hps * hd
    ng = MOBA_HEADS // hps
    nb = seq // blk
    assert nb % (2 * MOBA_GROUP) == 0
    slope_b = jnp.broadcast_to((slopes.astype(F32) * LOG2E)[:, None, None], (MOBA_HEADS, 1, blk))
    whole_seq = lambda part: pl.BlockSpec((seq, wide), lambda b, hg, i: (b, part * ng + hg),
                                          pipeline_mode=pl.Buffered(1))
    return pl.pallas_call(
        _moba_kernel,
        grid=(batch, ng, nb),
        in_specs=[
            pl.BlockSpec((blk, wide), lambda b, hg, i: (b * nb + i, hg)),
            whole_seq(1),
            whole_seq(2),
            pl.BlockSpec((hps, 1, blk), lambda b, hg, i: (hg, 0, 0)),
        ],
        out_specs=pl.BlockSpec((blk, wide), lambda b, hg, i: (b * nb + i, hg)),
        out_shape=jax.ShapeDtypeStruct((m, MOBA_HEADS * hd), BF16),
        scratch_shapes=[
            pltpu.VMEM((hps, seq, 2 * hd), BF16),
            pltpu.VMEM((hps, blk, 2 * hd), BF16),
            pltpu.VMEM((hps, nb, hd), F32),
            pltpu.VMEM((hps, 3 * nb, hd), BF16),
            pltpu.VMEM((hps, nb, blk), F32),
            pltpu.VMEM((hps * MOBA_GROUP, blk, blk), F32),
            pltpu.VMEM((hps * MOBA_GROUP, blk, blk), F32),
            pltpu.VMEM((hps, MOBA_VROWS, blk), F32),
            pltpu.VMEM((hps, nb, MOBA_VROWS, blk), BF16),
            pltpu.VMEM((hps * MOBA_GROUP, 1, blk), F32),
            pltpu.VMEM((hps * MOBA_GROUP, 1, blk), F32),
        ],
        compiler_params=_params(("parallel", "parallel", "arbitrary")),
        name="moba",
    )(proj_m, proj_m, proj_m, slope_b)


def _mem_attn_kernel(x_ref, gpre_ref, wq_ref, kv_ref, wo_ref, gpost_ref, o_ref):
    hd = HEAD_DIM
    width = MEM_HEADS * hd
    kv = kv_ref[...]
    step = x_ref.shape[0] // 2
    for r in range(2):
        rows = slice(r * step, (r + 1) * step)
        x = x_ref[rows, :]
        hn = _rms(x, gpre_ref[...]).astype(BF16)
        q = (_dot(hn, wq_ref[...]) * (hd ** -0.5)).astype(BF16)
        heads = []
        for hh in range(MEM_HEADS):
            s = _dot_nt(q[:, hh * hd:(hh + 1) * hd], kv[:, hh * hd:(hh + 1) * hd])
            p = jnp.exp(s - jnp.max(s, axis=-1, keepdims=True))
            p = p / jnp.sum(p, axis=-1, keepdims=True)
            heads.append(_dot(p.astype(BF16), kv[:, width + hh * hd:width + (hh + 1) * hd]))
        o = jnp.concatenate(heads, axis=-1).astype(BF16)
        o_ref[rows, :] = x + _rms(_dot(o, wo_ref[...]), gpost_ref[...])


def _mem_attn(x2d, g_pre, wq_bf16, kv_bf16, wo_bf16, g_post, batch, seq, *, tm):
    m, d = x2d.shape
    nt = seq // tm
    mem_len, kvw = kv_bf16.shape[1], kv_bf16.shape[2]
    width = wq_bf16.shape[1]
    return pl.pallas_call(
        _mem_attn_kernel,
        grid=(batch, nt),
        in_specs=[
            pl.BlockSpec((tm, d), lambda b, t: (b * nt + t, 0)),
            pl.BlockSpec((1, d), lambda b, t: (0, 0)),
            pl.BlockSpec((d, width), lambda b, t: (0, 0)),
            pl.BlockSpec((None, mem_len, kvw), lambda b, t: (b, 0, 0)),
            pl.BlockSpec((width, d), lambda b, t: (0, 0)),
            pl.BlockSpec((1, d), lambda b, t: (0, 0)),
        ],
        out_specs=pl.BlockSpec((tm, d), lambda b, t: (b * nt + t, 0)),
        out_shape=jax.ShapeDtypeStruct((m, d), F32),
        compiler_params=_params(("parallel", "parallel")),
        name="mem_attn",
    )(x2d, g_pre.reshape(1, d), wq_bf16, kv_bf16, wo_bf16, g_post.reshape(1, d))


def kernel(x, mem, pre_mix_norm, w_in, conv_w, a_log, dt_bias, gdn_norm_w, w_out, post_mix_norm,
           pre_mem_norm, mem_kv_norm, w_mq, w_mk, w_mv, w_mo, post_mem_norm,
           pre_mlp_norm, w_up, w_down, post_mlp_norm):
    batch, seq, d = x.shape
    mem_len = mem.shape[1]
    gw = GDN_HEADS * HEAD_DIM
    mw = MOBA_HEADS * HEAD_DIM
    assert seq % MOBA_BLOCK == 0 and seq % 512 == 0
    x2d = x.reshape(batch * seq, d)
    mem2d = mem.reshape(batch * mem_len, d)
    slopes = jnp.exp2(-8.0 * jnp.arange(1, MOBA_HEADS + 1, dtype=F32) / MOBA_HEADS)
    ones = lambda n: jnp.ones((n,), F32)

    for l in range(w_in.shape[0]):
        win = w_in[l]
        w_g = win[:, :4 * gw].astype(BF16)
        w_ab = jnp.pad(win[:, 4 * gw:4 * gw + 2 * GDN_HEADS], ((0, 0), (0, HEAD_DIM - 2 * GDN_HEADS))).astype(BF16)
        w_m = win[:, 4 * gw + 2 * GDN_HEADS:].astype(BF16)
        q_scale = jnp.concatenate([jnp.full((mw,), LOG2E * HEAD_DIM ** -0.5, F32), ones(2 * mw)])

        proj_g, proj_ab = _norm_matmul(x2d, pre_mix_norm[l], w_g, ones(4 * gw), F32, tm=1024, tn=1024,
                                       side_w=w_ab, name="in_proj_gdn")
        proj_m = _norm_matmul(x2d, pre_mix_norm[l], w_m, q_scale, BF16, tm=1024, tn=1024, name="in_proj_moba")

        y_gdn = _gdn(proj_g, proj_ab, conv_w[l], a_log[l], dt_bias[l], gdn_norm_w[l], batch, seq, tt=256, hps=8)
        y_moba = _moba(proj_m, slopes, batch, seq, hps=2)
        x2d = _matmul_norm_res([y_gdn, y_moba], w_out[l].astype(BF16), post_mix_norm[l], x2d,
                               tm=512, tk=gw + mw, name="out_proj")

        w_kv = jnp.concatenate([w_mk[l], w_mv[l]], axis=1).astype(BF16)
        kv = _norm_matmul(mem2d, mem_kv_norm[l], w_kv, ones(w_kv.shape[1]), BF16, tm=512, tn=1024, name="mem_kv")
        x2d = _mem_attn(x2d, pre_mem_norm[l], w_mq[l].astype(BF16), kv.reshape(batch, mem_len, -1),
                        w_mo[l].astype(BF16), post_mem_norm[l], batch, seq, tm=1024)

        hid = _norm_matmul(x2d, pre_mlp_norm[l], w_up[l].astype(BF16), ones(w_up.shape[2]), BF16,
                           tm=1024, tn=1024, relu2=True, name="mlp_up")
        x2d = _matmul_norm_res([hid], w_down[l].astype(BF16), post_mlp_norm[l], x2d, tm=512, tk=2048, name="mlp_down")
    return x2d.reshape(batch, seq, d)
```

```python
import functools
import math

import jax
import jax.numpy as jnp
from jax import lax
from jax.experimental import pallas as pl
from jax.experimental.pallas import tpu as pltpu

F32 = jnp.float32
BF16 = jnp.bfloat16

HEAD_DIM = 128
GDN_HEADS = 8
MOBA_HEADS = 8
CONV_WIDTH = 4
GDN_CHUNK = 64
GDN_SUB = 16
MOBA_BLOCK = 256
MOBA_TOPK = 3
MOBA_GROUP = 4
MOBA_VROWS = HEAD_DIM + 16
MEM_HEADS = 4
NORM_EPS = 1e-6
NORM_CHUNKS = 4
LOG2E = math.log2(math.e)

V7X_VMEM_BYTES = 64 * 1024 * 1024
VMEM_LIMIT = V7X_VMEM_BYTES - 8 * 1024 * 1024

NEG_INF = float("-inf")


def _params(semantics):
    return pltpu.CompilerParams(dimension_semantics=semantics, vmem_limit_bytes=VMEM_LIMIT)


def _dot(a, b):
    return jnp.dot(a, b, preferred_element_type=F32)


def _dot_nt(a, b):
    return lax.dot_general(a, b, (((1,), (1,)), ((), ())), preferred_element_type=F32)


def _dot_tn(a, b):
    return lax.dot_general(a, b, (((0,), (0,)), ((), ())), preferred_element_type=F32)


def _bdot(a, b):
    return _dot(a.astype(BF16), b.astype(BF16))


def _split3(x):
    hi = x.astype(BF16)
    rest = x - hi.astype(F32)
    mid = rest.astype(BF16)
    lo = (rest - mid.astype(F32)).astype(BF16)
    return hi, mid, lo


def _rms(y, gain):
    return y * lax.rsqrt(jnp.mean(y * y, axis=-1, keepdims=True) + NORM_EPS) * gain


def _norm_matmul_kernel(*refs, relu2, side):
    if side:
        x_ref, g_ref, w_ref, cs_ref, ws_ref, o_ref, os_ref, hn_ref = refs
    else:
        x_ref, g_ref, w_ref, cs_ref, o_ref, hn_ref = refs

    def project(hn, rows):
        y = _dot(hn, w_ref[...]) * cs_ref[...]
        if relu2:
            y = jnp.square(jnp.maximum(y, 0.0))
        o_ref[rows, :] = y.astype(o_ref.dtype)

    @pl.when(pl.program_id(1) == 0)
    def _():
        step = x_ref.shape[0] // NORM_CHUNKS
        for r in range(NORM_CHUNKS):
            rows = slice(r * step, (r + 1) * step)
            hn = _rms(x_ref[rows, :], g_ref[...]).astype(BF16)
            hn_ref[rows, :] = hn
            if side:
                os_ref[rows, :] = _dot(hn, ws_ref[...])
            project(hn, rows)

    @pl.when(pl.program_id(1) != 0)
    def _():
        project(hn_ref[...], slice(None))


def _norm_matmul(x2d, gain, w_bf16, col_scale, out_dtype, *, tm, tn, relu2=False, side_w=None, name):
    m, d = x2d.shape
    n = w_bf16.shape[1]
    tm, tn = min(tm, m), min(tn, n)
    side = side_w is not None
    in_specs = [
        pl.BlockSpec((tm, d), lambda i, j: (i, 0)),
        pl.BlockSpec((1, d), lambda i, j: (0, 0)),
        pl.BlockSpec((d, tn), lambda i, j: (0, j)),
        pl.BlockSpec((1, tn), lambda i, j: (0, j)),
    ]
    out_specs = pl.BlockSpec((tm, tn), lambda i, j: (i, j))
    out_shape = jax.ShapeDtypeStruct((m, n), out_dtype)
    args = [x2d, gain.reshape(1, d), w_bf16, col_scale.reshape(1, n)]
    if side:
        ns = side_w.shape[1]
        in_specs.append(pl.BlockSpec((d, ns), lambda i, j: (0, 0)))
        out_specs = [out_specs, pl.BlockSpec((tm, ns), lambda i, j: (i, 0))]
        out_shape = [out_shape, jax.ShapeDtypeStruct((m, ns), F32)]
        args.append(side_w)
    return pl.pallas_call(
        functools.partial(_norm_matmul_kernel, relu2=relu2, side=side),
        grid=(m // tm, n // tn),
        in_specs=in_specs,
        out_specs=out_specs,
        out_shape=out_shape,
        scratch_shapes=[pltpu.VMEM((tm, d), BF16)],
        compiler_params=_params(("parallel", "arbitrary")),
        name=name,
    )(*args)


def _matmul_norm_res_kernel(*refs, n_in, n_k):
    a_refs = refs[:n_in]
    w_ref, g_ref, r_ref, o_ref = refs[n_in:n_in + 4]

    def product(rows):
        off = 0
        acc = None
        for a_ref in a_refs:
            width = a_ref.shape[1]
            part = _dot(a_ref[rows, :], w_ref[off:off + width, :])
            acc = part if acc is None else acc + part
            off += width
        return acc

    if n_k == 1:
        o_ref[...] = r_ref[...] + _rms(product(slice(None)), g_ref[...])
        return

    acc_ref = refs[n_in + 4]
    k = pl.program_id(1)
    last = n_k - 1

    def finish():
        step = o_ref.shape[0] // NORM_CHUNKS
        for r in range(NORM_CHUNKS):
            rows = slice(r * step, (r + 1) * step)
            o_ref[rows, :] = r_ref[rows, :] + _rms(product(rows) + acc_ref[rows, :], g_ref[...])

    @pl.when(k == 0)
    def _():
        acc_ref[...] = product(slice(None))

    @pl.when((k != 0) & (k != last))
    def _():
        acc_ref[...] += product(slice(None))

    pl.when(k == last)(finish)


def _matmul_norm_res(a_list, w_bf16, gain, resid, *, tm, tk, name):
    m, d = resid.shape
    ktot = w_bf16.shape[0]
    n_in = len(a_list)
    tm = min(tm, m)
    if n_in > 1:
        tk = ktot
        a_specs = [pl.BlockSpec((tm, a.shape[1]), lambda i, k: (i, 0)) for a in a_list]
    else:
        tk = min(tk, ktot)
        a_specs = [pl.BlockSpec((tm, tk), lambda i, k: (i, k))]
    return pl.pallas_call(
        functools.partial(_matmul_norm_res_kernel, n_in=n_in, n_k=ktot // tk),
        grid=(m // tm, ktot // tk),
        in_specs=a_specs + [
            pl.BlockSpec((tk, d), lambda i, k: (k, 0)),
            pl.BlockSpec((1, d), lambda i, k: (0, 0)),
            pl.BlockSpec((tm, d), lambda i, k: (i, 0)),
        ],
        out_specs=pl.BlockSpec((tm, d), lambda i, k: (i, 0)),
        out_shape=jax.ShapeDtypeStruct((m, d), F32),
        scratch_shapes=[pltpu.VMEM((tm, d), F32)] if ktot > tk else [],
        compiler_params=_params(("parallel", "arbitrary")),
        name=name,
    )(*a_list, w_bf16, gain.reshape(1, d), resid)


def _gdn_kernel(q_ref, k_ref, v_ref, z_ref, ab_ref, cwq_ref, cwk_ref, cwv_ref, alog_ref, dtb_ref, nw_ref,
                o_ref, state_ref, tail_ref, pad_ref, qs_ref, ks_ref, vs_ref, gs_ref, bs_ref, os_ref,
                w2_ref, r_ref, qp_ref, op_ref, gl_ref):
    tt = q_ref.shape[0]
    c = GDN_CHUNK
    hd = HEAD_DIM
    hps = q_ref.shape[1] // hd
    n_chunks = tt // c
    head0 = pl.program_id(1) * hps
    lanes = lambda hh: slice(hh * hd, (hh + 1) * hd)

    @pl.when(pl.program_id(2) == 0)
    def _():
        state_ref[...] = jnp.zeros_like(state_ref)
        tail_ref[...] = jnp.zeros_like(tail_ref)

    def conv_silu(x_ref, slot, cw_ref):
        pad_ref[0:8, :] = tail_ref[slot]
        pad_ref[8:, :] = x_ref[...]
        tail_ref[slot] = x_ref[tt - 8:tt, :]
        y = cw_ref[0:1, :] * pad_ref[pl.ds(8 - (CONV_WIDTH - 1), tt), :]
        for j in range(1, CONV_WIDTH):
            y = y + cw_ref[j:j + 1, :] * pad_ref[pl.ds(8 - (CONV_WIDTH - 1) + j, tt), :]
        return y * jax.nn.sigmoid(y)

    def l2n(y):
        return y * lax.rsqrt(jnp.sum(y * y, axis=-1, keepdims=True) + NORM_EPS)

    qc = conv_silu(q_ref, 0, cwq_ref)
    kc = conv_silu(k_ref, 1, cwk_ref)
    vs_ref[...] = conv_silu(v_ref, 2, cwv_ref)
    ab = ab_ref[...]
    lane = lax.broadcasted_iota(jnp.int32, ab.shape, 1)
    xa = ab + dtb_ref[...]
    softplus = jnp.maximum(xa, 0.0) + jnp.log(1.0 + jnp.exp(-jnp.abs(xa)))
    g_all = -(jnp.exp(alog_ref[...]) * softplus)
    beta_all = jax.nn.sigmoid(ab)
    for hh in range(hps):
        qs_ref[:, lanes(hh)] = l2n(qc[:, lanes(hh)]) * (HEAD_DIM ** -0.5)
        ks_ref[:, lanes(hh)] = l2n(kc[:, lanes(hh)])
        g_col = jnp.sum(jnp.where(lane == head0 + hh, g_all, 0.0), axis=-1, keepdims=True)
        b_col = jnp.sum(jnp.where(lane == head0 + hh + GDN_HEADS, beta_all, 0.0), axis=-1, keepdims=True)
        gs_ref[:, lanes(hh)] = jnp.broadcast_to(g_col, (tt, hd))
        bs_ref[:, lanes(hh)] = jnp.broadcast_to(b_col, (tt, hd))

    row = lax.broadcasted_iota(jnp.int32, (c, c), 0)
    col = lax.broadcasted_iota(jnp.int32, (c, c), 1)
    causal = row >= col
    strict = row > col
    same_sub = (row // GDN_SUB) == (col // GDN_SUB)
    ltri3 = jnp.concatenate([causal.astype(BF16)] * 3, axis=1)

    units = [(n, hh) for n in range(n_chunks) for hh in range(hps)]
    each = lambda fn, *cols: [fn(*args) for args in zip(*cols)]
    rows = lambda ref: [ref[n * c:(n + 1) * c, lanes(hh)] for n, hh in units]
    q, k, v, g, beta = rows(qs_ref), rows(ks_ref), rows(vs_ref), rows(gs_ref), rows(bs_ref)
    pieces = each(_split3, g)
    gc = each(lambda p: _dot(ltri3, jnp.concatenate(p, axis=0)), pieces)
    dexp = each(lambda p: _dot(ltri3, jnp.concatenate([jnp.where(strict, x[:, :c], 0.0) for x in p], axis=0)), pieces)
    decay = each(lambda d: jnp.where(causal, jnp.exp(d), 0.0), dexp)
    kb = each(lambda a, b: a * b, k, beta)
    kbf = each(lambda a: a.astype(BF16), k)
    mm = each(lambda a, b, d: jnp.where(strict, _dot_nt(a.astype(BF16), b) * d, 0.0), kb, kbf, decay)
    a_qk = each(lambda a, b, d: (_dot_nt(a.astype(BF16), b) * d).astype(BF16), q, kbf, decay)
    x = each(lambda m_: jnp.where(same_sub, -m_, 0.0), mm)
    lo = each(lambda m_: jnp.where(same_sub, 0.0, m_), mm)
    x2 = each(_bdot, x, x)
    x4 = each(_bdot, x2, x2)
    a1 = each(lambda a, b: a + b + _bdot(a, b), x, x2)
    x8 = each(_bdot, x4, x4)
    a2 = each(lambda a, b: a + b + _bdot(a, b), a1, x4)
    dm = each(lambda a, b: a + b + _bdot(a, b), a2, x8)
    y = each(lambda d, l_: -(l_ + _bdot(d, l_)), dm, lo)
    y2 = each(_bdot, y, y)
    qm = each(lambda a, b: a + b + _bdot(a, b), y, y2)
    tm = each(lambda a, b: a + b + _bdot(a, b), qm, dm)
    eg = each(jnp.exp, gc)
    uw = each(lambda v_, b_, kb_, e_: jnp.concatenate([v_ * b_, kb_ * e_], axis=1), v, beta, kb, eg)
    uw = each(lambda t_, u_: (u_ + _bdot(t_, u_)).astype(BF16), tm, uw)
    au = each(_dot, a_qk, uw)
    kt = each(lambda k_, g_: (k_ * jnp.exp(g_[c - 1:c, :] - g_)).astype(BF16), k, gc)
    kw = each(_dot_tn, kt, uw)
    for u in range(len(units)):
        op_ref[u] = au[u][:, :hd]
        qp_ref[u] = (q[u] * eg[u] - au[u][:, hd:]).astype(BF16)
        r_ref[u] = kw[u][:, :hd]
        w2_ref[u] = kw[u][:, hd:].astype(BF16)
        gl_ref[u] = jnp.exp(gc[u][c - 1:c, :])

    s = [state_ref[hh] for hh in range(hps)]
    for u, (n, hh) in enumerate(units):
        sb = s[hh].astype(BF16)
        os_ref[n * c:(n + 1) * c, lanes(hh)] = op_ref[u] + _dot(qp_ref[u], sb)
        s[hh] = s[hh] * gl_ref[u] + r_ref[u] - _dot(w2_ref[u], sb)
    for hh in range(hps):
        state_ref[hh] = s[hh]
        z = z_ref[:, lanes(hh)]
        o_ref[:, lanes(hh)] = (_rms(os_ref[:, lanes(hh)], nw_ref[...]) * (z * jax.nn.sigmoid(z))).astype(o_ref.dtype)


def _gdn(proj_g, proj_ab, conv_w, a_log, dt_bias, norm_w, batch, seq, *, tt, hps):
    m = proj_g.shape[0]
    nt = seq // tt
    hd = HEAD_DIM
    wide = hps * hd
    ng = GDN_HEADS // hps
    n_units = (tt // GDN_CHUNK) * hps
    row = lambda b, hg, t: b * nt + t
    on_a_lanes = lambda p: jnp.pad(p.astype(F32), (0, hd - GDN_HEADS)).reshape(1, hd)
    tile = lambda part: pl.BlockSpec((tt, wide), lambda b, hg, t: (row(b, hg, t), part * ng + hg))
    cw = lambda part: pl.BlockSpec((CONV_WIDTH, wide), lambda b, hg, t: (0, part * ng + hg))
    per_head = pl.BlockSpec((1, hd), lambda b, hg, t: (0, 0))
    return pl.pallas_call(
        _gdn_kernel,
        grid=(batch, ng, nt),
        in_specs=[
            tile(0), tile(1), tile(2), tile(3),
            pl.BlockSpec((tt, hd), lambda b, hg, t: (row(b, hg, t), 0)),
            cw(0), cw(1), cw(2),
            per_head, per_head,
            pl.BlockSpec((1, hd), lambda b, hg, t: (0, 0)),
        ],
        out_specs=pl.BlockSpec((tt, wide), lambda b, hg, t: (row(b, hg, t), hg)),
        out_shape=jax.ShapeDtypeStruct((m, GDN_HEADS * hd), BF16),
        scratch_shapes=[
            pltpu.VMEM((hps, hd, hd), F32),
            pltpu.VMEM((3, 8, wide), F32),
            pltpu.VMEM((tt + 8, wide), F32),
            pltpu.VMEM((tt, wide), F32),
            pltpu.VMEM((tt, wide), F32),
            pltpu.VMEM((tt, wide), F32),
            pltpu.VMEM((tt, wide), F32),
            pltpu.VMEM((tt, wide), F32),
            pltpu.VMEM((tt, wide), F32),
            pltpu.VMEM((n_units, hd, hd), BF16),
            pltpu.VMEM((n_units, hd, hd), F32),
            pltpu.VMEM((n_units, GDN_CHUNK, hd), BF16),
            pltpu.VMEM((n_units, GDN_CHUNK, hd), F32),
            pltpu.VMEM((n_units, 1, hd), F32),
        ],
        compiler_params=_params(("parallel", "parallel", "arbitrary")),
        name="gdn",
    )(proj_g, proj_g, proj_g, proj_g, proj_ab, conv_w, conv_w, conv_w,
      on_a_lanes(a_log), on_a_lanes(dt_bias), norm_w.reshape(1, hd))


def _moba_kernel(q_ref, k_ref, v_ref, slope_ref, o_ref,
                 kaug_ref, qaug_ref, kmean_ref, kparts_ref, sel_ref, sa_ref, sb_ref, acc_ref, vt_ref, ta_ref, tb_ref):
    blk = MOBA_BLOCK
    hd = HEAD_DIM
    grp = MOBA_GROUP
    hps = q_ref.shape[1] // hd
    heads = range(hps)
    nb = k_ref.shape[0] // blk
    i = pl.program_id(2)
    lanes = lambda hh: slice(hh * hd, (hh + 1) * hd)
    slope2 = [slope_ref[hh] for hh in heads]

    @pl.when(i == 0)
    def _():
        pos = lax.broadcasted_iota(jnp.int32, (blk, hd), 0).astype(F32)
        lane = lax.broadcasted_iota(jnp.int32, (blk, hd), 1)
        one = jnp.ones((blk, hd), BF16)
        zero = jnp.zeros((blk, hd), BF16)
        ones_row = (lax.broadcasted_iota(jnp.int32, (MOBA_VROWS - hd, blk), 0) == 0).astype(BF16)
        k_extra = []
        for hh in heads:
            hi, mid, lo = _split3(slope2[hh][:, :hd] * pos)
            pieces = jnp.where(lane == 0, hi, jnp.where(lane == 1, mid, jnp.where(lane == 2, lo, zero)))
            k_extra.append(jnp.where((lane >= 3) & (lane < 6), one, pieces))
            npieces = jnp.where(lane == 3, -hi, jnp.where(lane == 4, -mid, jnp.where(lane == 5, -lo, zero)))
            qaug_ref[hh, :, hd:] = jnp.where(lane < 3, one, npieces)

        def block_step(j, carry):
            r0 = pl.multiple_of(j * blk, blk)
            for hh in heads:
                kb = k_ref[pl.ds(r0, blk), lanes(hh)]
                kaug_ref[hh, pl.ds(r0, blk), 0:hd] = kb
                kaug_ref[hh, pl.ds(r0, blk), hd:] = k_extra[hh]
                kmean_ref[hh, pl.ds(j, 1), :] = jnp.mean(kb.astype(F32), axis=0, keepdims=True)
                vt_ref[hh, j, 0:hd, :] = v_ref[pl.ds(r0, blk), lanes(hh)].astype(F32).T.astype(BF16)
                vt_ref[hh, j, hd:, :] = ones_row
            return carry
        lax.fori_loop(0, nb, block_step, 0, unroll=2)
        for hh in heads:
            for part, piece in enumerate(_split3(kmean_ref[hh])):
                kparts_ref[hh, part * nb:(part + 1) * nb, :] = piece

    blk_id = lax.broadcasted_iota(jnp.int32, (nb, blk), 0)
    key_pos = lax.broadcasted_iota(jnp.int32, (blk, blk), 0)
    qry_pos = lax.broadcasted_iota(jnp.int32, (blk, blk), 1)
    r0 = pl.multiple_of(i * blk, blk)

    def score_group(g, dst_ref, top_ref):
        for b in range(grp):
            c0 = pl.multiple_of((g * grp + b) * blk, blk)
            for hh in heads:
                s2 = _dot_nt(kaug_ref[hh, pl.ds(c0, blk), :], qa[hh])
                dst_ref[hh * grp + b] = s2
                top_ref[hh * grp + b] = jnp.max(s2, axis=0, keepdims=True)

    qa, gates, own = [], [], []
    for hh in heads:
        q = q_ref[:, lanes(hh)]
        qaug_ref[hh, :, 0:hd] = q
        qa.append(qaug_ref[hh])
        g3 = _dot_nt(kparts_ref[hh], q)
        gates.append(g3[0:nb] + g3[nb:2 * nb] + g3[2 * nb:3 * nb])
        own.append(_dot_nt(kaug_ref[hh, pl.ds(r0, blk), :], qa[hh]))
    score_group(0, sa_ref, ta_ref)
    score_group(1, sb_ref, tb_ref)

    start = []
    for hh in heads:
        gate = jnp.where(blk_id < i, gates[hh], NEG_INF)
        sel = jnp.zeros((nb, blk), F32)
        for _ in range(MOBA_TOPK):
            best = jnp.max(gate, axis=0, keepdims=True)
            hit = (gate == best) & (best > NEG_INF)
            first = jnp.min(jnp.where(hit, blk_id, nb), axis=0, keepdims=True)
            pick = blk_id == first
            sel = jnp.where(pick, 1.0, sel)
            gate = jnp.where(pick, NEG_INF, gate)
        sel_ref[hh] = sel
        s = jnp.where(key_pos <= qry_pos, own[hh], NEG_INF)
        m0 = jnp.max(s, axis=0, keepdims=True)
        p = jnp.exp2(s - m0)
        acc_ref[hh] = _dot(vt_ref[hh, i], p.astype(BF16))
        start.append(m0)

    def group_max(g, top_ref, carry):
        subs = [[] for _ in heads]
        m_new = [carry[hh] for hh in heads]
        for b in range(grp):
            j = g * grp + b
            for hh in heads:
                chosen = sel_ref[hh, pl.ds(j, 1), :] > 0.0
                off = slope2[hh] * ((i - j) * blk).astype(F32)
                top = top_ref[hh * grp + b] - off
                m_new[hh] = jnp.maximum(m_new[hh], jnp.where(chosen, top, NEG_INF))
                subs[hh].append((chosen, off))
        return m_new, subs

    last_group = nb // grp - 1
    n_groups = (i + grp - 1) // grp
    n_pairs = n_groups // 2

    def attend_group(g, src_ref, top_ref, carry, refill):
        m_new, subs = group_max(g, top_ref, carry)
        alpha = [jnp.exp2(carry[hh] - m_new[hh]) for hh in heads]
        nxt = jnp.minimum(g + 2, last_group)
        pv = [None for _ in heads]
        for b in range(grp):
            c0 = pl.multiple_of((nxt * grp + b) * blk, blk)
            for hh in heads:
                chosen, off = subs[hh][b]
                p2 = jnp.exp2((src_ref[hh * grp + b] - jnp.where(chosen, m_new[hh] + off, jnp.inf)).astype(BF16))
                part = _dot(vt_ref[hh, g * grp + b], p2)
                pv[hh] = part if pv[hh] is None else pv[hh] + part
                if refill:
                    s2 = _dot_nt(kaug_ref[hh, pl.ds(c0, blk), :], qa[hh])
                    src_ref[hh * grp + b] = s2
                    top_ref[hh * grp + b] = jnp.max(s2, axis=0, keepdims=True)
        for hh in heads:
            acc_ref[hh] = acc_ref[hh] * alpha[hh] + pv[hh]
        return tuple(m_new)

    def past_pair(gp, carry, refill_odd=True):
        carry = attend_group(2 * gp, sa_ref, ta_ref, carry, True)
        return attend_group(2 * gp + 1, sb_ref, tb_ref, carry, refill_odd)

    final = lax.fori_loop(0, jnp.maximum(n_pairs - 1, 0), past_pair, tuple(start))
    final = lax.cond(n_pairs > 0, lambda c: past_pair(n_pairs - 1, c, False), lambda c: c, final)

    @pl.when(n_groups % 2 == 1)
    def _():
        attend_group(n_groups - 1, sa_ref, ta_ref, final, False)

    for hh in heads:
        o_ref[:, lanes(hh)] = (acc_ref[hh, 0:hd, :] / acc_ref[hh, hd:hd + 1, :]).T.astype(o_ref.dtype)


def _moba(proj_m, slopes, batch, seq, *, hps):
    m = proj_m.shape[0]
    blk, hd = MOBA_BLOCK, HEAD_DIM
    wide = hps * hd
    ng = MOBA_HEADS // hps
    nb = seq // blk
    assert nb % (2 * MOBA_GROUP) == 0
    slope_b = jnp.broadcast_to((slopes.astype(F32) * LOG2E)[:, None, None], (MOBA_HEADS, 1, blk))
    whole_seq = lambda part: pl.BlockSpec((seq, wide), lambda b, hg, i: (b, part * ng + hg),
                                          pipeline_mode=pl.Buffered(1))
    return pl.pallas_call(
        _moba_kernel,
        grid=(batch, ng, nb),
        in_specs=[
            pl.BlockSpec((blk, wide), lambda b, hg, i: (b * nb + i, hg)),
            whole_seq(1),
            whole_seq(2),
            pl.BlockSpec((hps, 1, blk), lambda b, hg, i: (hg, 0, 0)),
        ],
        out_specs=pl.BlockSpec((blk, wide), lambda b, hg, i: (b * nb + i, hg)),
        out_shape=jax.ShapeDtypeStruct((m, MOBA_HEADS * hd), BF16),
        scratch_shapes=[
            pltpu.VMEM((hps, seq, 2 * hd), BF16),
            pltpu.VMEM((hps, blk, 2 * hd), BF16),
            pltpu.VMEM((hps, nb, hd), F32),
            pltpu.VMEM((hps, 3 * nb, hd), BF16),
            pltpu.VMEM((hps, nb, blk), F32),
            pltpu.VMEM((hps * MOBA_GROUP, blk, blk), F32),
            pltpu.VMEM((hps * MOBA_GROUP, blk, blk), F32),
            pltpu.VMEM((hps, MOBA_VROWS, blk), F32),
            pltpu.VMEM((hps, nb, MOBA_VROWS, blk), BF16),
            pltpu.VMEM((hps * MOBA_GROUP, 1, blk), F32),
            pltpu.VMEM((hps * MOBA_GROUP, 1, blk), F32),
        ],
        compiler_params=_params(("parallel", "parallel", "arbitrary")),
        name="moba",
    )(proj_m, proj_m, proj_m, slope_b)


def _mem_attn_kernel(x_ref, gpre_ref, wq_ref, kv_ref, wo_ref, gpost_ref, o_ref):
    hd = HEAD_DIM
    width = MEM_HEADS * hd
    kv = kv_ref[...]
    step = x_ref.shape[0] // 2
    for r in range(2):
        rows = slice(r * step, (r + 1) * step)
        x = x_ref[rows, :]
        hn = _rms(x, gpre_ref[...]).astype(BF16)
        q = (_dot(hn, wq_ref[...]) * (hd ** -0.5)).astype(BF16)
        heads = []
        for hh in range(MEM_HEADS):
            s = _dot_nt(q[:, hh * hd:(hh + 1) * hd], kv[:, hh * hd:(hh + 1) * hd])
            p = jnp.exp(s - jnp.max(s, axis=-1, keepdims=True))
            p = p / jnp.sum(p, axis=-1, keepdims=True)
            heads.append(_dot(p.astype(BF16), kv[:, width + hh * hd:width + (hh + 1) * hd]))
        o = jnp.concatenate(heads, axis=-1).astype(BF16)
        o_ref[rows, :] = x + _rms(_dot(o, wo_ref[...]), gpost_ref[...])


def _mem_attn(x2d, g_pre, wq_bf16, kv_bf16, wo_bf16, g_post, batch, seq, *, tm):
    m, d = x2d.shape
    nt = seq // tm
    mem_len, kvw = kv_bf16.shape[1], kv_bf16.shape[2]
    width = wq_bf16.shape[1]
    return pl.pallas_call(
        _mem_attn_kernel,
        grid=(batch, nt),
        in_specs=[
            pl.BlockSpec((tm, d), lambda b, t: (b * nt + t, 0)),
            pl.BlockSpec((1, d), lambda b, t: (0, 0)),
            pl.BlockSpec((d, width), lambda b, t: (0, 0)),
            pl.BlockSpec((None, mem_len, kvw), lambda b, t: (b, 0, 0)),
            pl.BlockSpec((width, d), lambda b, t: (0, 0)),
            pl.BlockSpec((1, d), lambda b, t: (0, 0)),
        ],
        out_specs=pl.BlockSpec((tm, d), lambda b, t: (b * nt + t, 0)),
        out_shape=jax.ShapeDtypeStruct((m, d), F32),
        compiler_params=_params(("parallel", "parallel")),
        name="mem_attn",
    )(x2d, g_pre.reshape(1, d), wq_bf16, kv_bf16, wo_bf16, g_post.reshape(1, d))


def kernel(x, mem, pre_mix_norm, w_in, conv_w, a_log, dt_bias, gdn_norm_w, w_out, post_mix_norm,
           pre_mem_norm, mem_kv_norm, w_mq, w_mk, w_mv, w_mo, post_mem_norm,
           pre_mlp_norm, w_up, w_down, post_mlp_norm):
    batch, seq, d = x.shape
    mem_len = mem.shape[1]
    gw = GDN_HEADS * HEAD_DIM
    mw = MOBA_HEADS * HEAD_DIM
    assert seq % MOBA_BLOCK == 0 and seq % 512 == 0
    x2d = x.reshape(batch * seq, d)
    mem2d = mem.reshape(batch * mem_len, d)
    slopes = jnp.exp2(-8.0 * jnp.arange(1, MOBA_HEADS + 1, dtype=F32) / MOBA_HEADS)
    ones = lambda n: jnp.ones((n,), F32)

    for l in range(w_in.shape[0]):
        win = w_in[l]
        w_g = win[:, :4 * gw].astype(BF16)
        w_ab = jnp.pad(win[:, 4 * gw:4 * gw + 2 * GDN_HEADS], ((0, 0), (0, HEAD_DIM - 2 * GDN_HEADS))).astype(BF16)
        w_m = win[:, 4 * gw + 2 * GDN_HEADS:].astype(BF16)
        q_scale = jnp.concatenate([jnp.full((mw,), LOG2E * HEAD_DIM ** -0.5, F32), ones(2 * mw)])

        proj_g, proj_ab = _norm_matmul(x2d, pre_mix_norm[l], w_g, ones(4 * gw), F32, tm=1024, tn=1024,
                                       side_w=w_ab, name="in_proj_gdn")
        proj_m = _norm_matmul(x2d, pre_mix_norm[l], w_m, q_scale, BF16, tm=1024, tn=1024, name="in_proj_moba")

        y_gdn = _gdn(proj_g, proj_ab, conv_w[l], a_log[l], dt_bias[l], gdn_norm_w[l], batch, seq, tt=256, hps=8)
        y_moba = _moba(proj_m, slopes, batch, seq, hps=2)
        x2d = _matmul_norm_res([y_gdn, y_moba], w_out[l].astype(BF16), post_mix_norm[l], x2d,
                               tm=512, tk=gw + mw, name="out_proj")

        w_kv = jnp.concatenate([w_mk[l], w_mv[l]], axis=1).astype(BF16)
        kv = _norm_matmul(mem2d, mem_kv_norm[l], w_kv, ones(w_kv.shape[1]), BF16, tm=512, tn=1024, name="mem_kv")
        x2d = _mem_attn(x2d, pre_mem_norm[l], w_mq[l].astype(BF16), kv.reshape(batch, mem_len, -1),
                        w_mo[l].astype(BF16), post_mem_norm[l], batch, seq, tm=1024)

        hid = _norm_matmul(x2d, pre_mlp_norm[l], w_up[l].astype(BF16), ones(w_up.shape[2]), BF16,
                           tm=1024, tn=1024, relu2=True, name="mlp_up")
        x2d = _matmul_norm_res([hid], w_down[l].astype(BF16), post_mlp_norm[l], x2d, tm=512, tk=2048, name="mlp_down")
    return x2d.reshape(batch, seq, d)
```

```python
import functools
import math

import jax
import jax.numpy as jnp
from jax import lax
from jax.experimental import pallas as pl
from jax.experimental.pallas import tpu as pltpu

F32 = jnp.float32
BF16 = jnp.bfloat16

HEAD_DIM = 128
GDN_HEADS = 8
MOBA_HEADS = 8
CONV_WIDTH = 4
GDN_CHUNK = 64
GDN_SUB = 16
MOBA_BLOCK = 256
MOBA_TOPK = 3
MOBA_GROUP = 4
MOBA_VROWS = HEAD_DIM + 16
MEM_HEADS = 4
NORM_EPS = 1e-6
NORM_CHUNKS = 4
LOG2E = math.log2(math.e)

V7X_VMEM_BYTES = 64 * 1024 * 1024
VMEM_LIMIT = V7X_VMEM_BYTES - 8 * 1024 * 1024

NEG_INF = float("-inf")


def _params(semantics):
    return pltpu.CompilerParams(dimension_semantics=semantics, vmem_limit_bytes=VMEM_LIMIT)


def _dot(a, b):
    return jnp.dot(a, b, preferred_element_type=F32)


def _dot_nt(a, b):
    return lax.dot_general(a, b, (((1,), (1,)), ((), ())), preferred_element_type=F32)


def _dot_tn(a, b):
    return lax.dot_general(a, b, (((0,), (0,)), ((), ())), preferred_element_type=F32)


def _bdot(a, b):
    return _dot(a.astype(BF16), b.astype(BF16))


def _split3(x):
    hi = x.astype(BF16)
    rest = x - hi.astype(F32)
    mid = rest.astype(BF16)
    lo = (rest - mid.astype(F32)).astype(BF16)
    return hi, mid, lo


def _rms(y, gain):
    return y * lax.rsqrt(jnp.mean(y * y, axis=-1, keepdims=True) + NORM_EPS) * gain


def _norm_matmul_kernel(*refs, relu2, side):
    if side:
        x_ref, g_ref, w_ref, cs_ref, ws_ref, o_ref, os_ref, hn_ref = refs
    else:
        x_ref, g_ref, w_ref, cs_ref, o_ref, hn_ref = refs

    def project(hn, rows):
        y = _dot(hn, w_ref[...]) * cs_ref[...]
        if relu2:
            y = jnp.square(jnp.maximum(y, 0.0))
        o_ref[rows, :] = y.astype(o_ref.dtype)

    @pl.when(pl.program_id(1) == 0)
    def _():
        step = x_ref.shape[0] // NORM_CHUNKS
        for r in range(NORM_CHUNKS):
            rows = slice(r * step, (r + 1) * step)
            hn = _rms(x_ref[rows, :], g_ref[...]).astype(BF16)
            hn_ref[rows, :] = hn
            if side:
                os_ref[rows, :] = _dot(hn, ws_ref[...])
            project(hn, rows)

    @pl.when(pl.program_id(1) != 0)
    def _():
        project(hn_ref[...], slice(None))


def _norm_matmul(x2d, gain, w_bf16, col_scale, out_dtype, *, tm, tn, relu2=False, side_w=None, name):
    m, d = x2d.shape
    n = w_bf16.shape[1]
    tm, tn = min(tm, m), min(tn, n)
    side = side_w is not None
    in_specs = [
        pl.BlockSpec((tm, d), lambda i, j: (i, 0)),
        pl.BlockSpec((1, d), lambda i, j: (0, 0)),
        pl.BlockSpec((d, tn), lambda i, j: (0, j)),
        pl.BlockSpec((1, tn), lambda i, j: (0, j)),
    ]
    out_specs = pl.BlockSpec((tm, tn), lambda i, j: (i, j))
    out_shape = jax.ShapeDtypeStruct((m, n), out_dtype)
    args = [x2d, gain.reshape(1, d), w_bf16, col_scale.reshape(1, n)]
    if side:
        ns = side_w.shape[1]
        in_specs.append(pl.BlockSpec((d, ns), lambda i, j: (0, 0)))
        out_specs = [out_specs, pl.BlockSpec((tm, ns), lambda i, j: (i, 0))]
        out_shape = [out_shape, jax.ShapeDtypeStruct((m, ns), F32)]
        args.append(side_w)
    return pl.pallas_call(
        functools.partial(_norm_matmul_kernel, relu2=relu2, side=side),
        grid=(m // tm, n // tn),
        in_specs=in_specs,
        out_specs=out_specs,
        out_shape=out_shape,
        scratch_shapes=[pltpu.VMEM((tm, d), BF16)],
        compiler_params=_params(("parallel", "arbitrary")),
        name=name,
    )(*args)


def _matmul_norm_res_kernel(*refs, n_in, n_k):
    a_refs = refs[:n_in]
    w_ref, g_ref, r_ref, o_ref = refs[n_in:n_in + 4]

    def product(rows):
        off = 0
        acc = None
        for a_ref in a_refs:
            width = a_ref.shape[1]
            part = _dot(a_ref[rows, :], w_ref[off:off + width, :])
            acc = part if acc is None else acc + part
            off += width
        return acc

    if n_k == 1:
        o_ref[...] = r_ref[...] + _rms(product(slice(None)), g_ref[...])
        return

    acc_ref = refs[n_in + 4]
    k = pl.program_id(1)
    last = n_k - 1

    def finish():
        step = o_ref.shape[0] // NORM_CHUNKS
        for r in range(NORM_CHUNKS):
            rows = slice(r * step, (r + 1) * step)
            o_ref[rows, :] = r_ref[rows, :] + _rms(product(rows) + acc_ref[rows, :], g_ref[...])

    @pl.when(k == 0)
    def _():
        acc_ref[...] = product(slice(None))

    @pl.when((k != 0) & (k != last))
    def _():
        acc_ref[...] += product(slice(None))

    pl.when(k == last)(finish)


def _matmul_norm_res(a_list, w_bf16, gain, resid, *, tm, tk, name):
    m, d = resid.shape
    ktot = w_bf16.shape[0]
    n_in = len(a_list)
    tm = min(tm, m)
    if n_in > 1:
        tk = ktot
        a_specs = [pl.BlockSpec((tm, a.shape[1]), lambda i, k: (i, 0)) for a in a_list]
    else:
        tk = min(tk, ktot)
        a_specs = [pl.BlockSpec((tm, tk), lambda i, k: (i, k))]
    return pl.pallas_call(
        functools.partial(_matmul_norm_res_kernel, n_in=n_in, n_k=ktot // tk),
        grid=(m // tm, ktot // tk),
        in_specs=a_specs + [
            pl.BlockSpec((tk, d), lambda i, k: (k, 0)),
            pl.BlockSpec((1, d), lambda i, k: (0, 0)),
            pl.BlockSpec((tm, d), lambda i, k: (i, 0)),
        ],
        out_specs=pl.BlockSpec((tm, d), lambda i, k: (i, 0)),
        out_shape=jax.ShapeDtypeStruct((m, d), F32),
        scratch_shapes=[pltpu.VMEM((tm, d), F32)] if ktot > tk else [],
        compiler_params=_params(("parallel", "arbitrary")),
        name=name,
    )(*a_list, w_bf16, gain.reshape(1, d), resid)


def _gdn_kernel(q_ref, k_ref, v_ref, z_ref, ab_ref, cwq_ref, cwk_ref, cwv_ref, alog_ref, dtb_ref, nw_ref,
                o_ref, state_ref, tail_ref, pad_ref, qs_ref, ks_ref, vs_ref, gs_ref, bs_ref, os_ref,
                w2_ref, r_ref, qp_ref, op_ref, gl_ref):
    tt = q_ref.shape[0]
    c = GDN_CHUNK
    hd = HEAD_DIM
    hps = q_ref.shape[1] // hd
    n_chunks = tt // c
    head0 = pl.program_id(1) * hps
    lanes = lambda hh: slice(hh * hd, (hh + 1) * hd)

    @pl.when(pl.program_id(2) == 0)
    def _():
        state_ref[...] = jnp.zeros_like(state_ref)
        tail_ref[...] = jnp.zeros_like(tail_ref)

    def conv_silu(x_ref, slot, cw_ref):
        pad_ref[0:8, :] = tail_ref[slot]
        pad_ref[8:, :] = x_ref[...]
        tail_ref[slot] = x_ref[tt - 8:tt, :]
        y = cw_ref[0:1, :] * pad_ref[pl.ds(8 - (CONV_WIDTH - 1), tt), :]
        for j in range(1, CONV_WIDTH):
            y = y + cw_ref[j:j + 1, :] * pad_ref[pl.ds(8 - (CONV_WIDTH - 1) + j, tt), :]
        return y * jax.nn.sigmoid(y)

    def l2n(y):
        return y * lax.rsqrt(jnp.sum(y * y, axis=-1, keepdims=True) + NORM_EPS)

    qc = conv_silu(q_ref, 0, cwq_ref)
    kc = conv_silu(k_ref, 1, cwk_ref)
    vs_ref[...] = conv_silu(v_ref, 2, cwv_ref)
    ab = ab_ref[...]
    lane = lax.broadcasted_iota(jnp.int32, ab.shape, 1)
    xa = ab + dtb_ref[...]
    softplus = jnp.maximum(xa, 0.0) + jnp.log(1.0 + jnp.exp(-jnp.abs(xa)))
    g_all = -(jnp.exp(alog_ref[...]) * softplus)
    beta_all = jax.nn.sigmoid(ab)
    for hh in range(hps):
        qs_ref[:, lanes(hh)] = l2n(qc[:, lanes(hh)]) * (HEAD_DIM ** -0.5)
        ks_ref[:, lanes(hh)] = l2n(kc[:, lanes(hh)])
        g_col = jnp.sum(jnp.where(lane == head0 + hh, g_all, 0.0), axis=-1, keepdims=True)
        b_col = jnp.sum(jnp.where(lane == head0 + hh + GDN_HEADS, beta_all, 0.0), axis=-1, keepdims=True)
        gs_ref[:, lanes(hh)] = jnp.broadcast_to(g_col, (tt, hd))
        bs_ref[:, lanes(hh)] = jnp.broadcast_to(b_col, (tt, hd))

    row = lax.broadcasted_iota(jnp.int32, (c, c), 0)
    col = lax.broadcasted_iota(jnp.int32, (c, c), 1)
    causal = row >= col
    strict = row > col
    same_sub = (row // GDN_SUB) == (col // GDN_SUB)
    ltri3 = jnp.concatenate([causal.astype(BF16)] * 3, axis=1)

    units = [(n, hh) for n in range(n_chunks) for hh in range(hps)]
    each = lambda fn, *cols: [fn(*args) for args in zip(*cols)]
    rows = lambda ref: [ref[n * c:(n + 1) * c, lanes(hh)] for n, hh in units]
    q, k, v, g, beta = rows(qs_ref), rows(ks_ref), rows(vs_ref), rows(gs_ref), rows(bs_ref)
    pieces = each(_split3, g)
    gc = each(lambda p: _dot(ltri3, jnp.concatenate(p, axis=0)), pieces)
    dexp = each(lambda p: _dot(ltri3, jnp.concatenate([jnp.where(strict, x[:, :c], 0.0) for x in p], axis=0)), pieces)
    decay = each(lambda d: jnp.where(causal, jnp.exp(d), 0.0), dexp)
    kb = each(lambda a, b: a * b, k, beta)
    kbf = each(lambda a: a.astype(BF16), k)
    mm = each(lambda a, b, d: jnp.where(strict, _dot_nt(a.astype(BF16), b) * d, 0.0), kb, kbf, decay)
    a_qk = each(lambda a, b, d: (_dot_nt(a.astype(BF16), b) * d).astype(BF16), q, kbf, decay)
    x = each(lambda m_: jnp.where(same_sub, -m_, 0.0), mm)
    lo = each(lambda m_: jnp.where(same_sub, 0.0, m_), mm)
    x2 = each(_bdot, x, x)
    x4 = each(_bdot, x2, x2)
    a1 = each(lambda a, b: a + b + _bdot(a, b), x, x2)
    x8 = each(_bdot, x4, x4)
    a2 = each(lambda a, b: a + b + _bdot(a, b), a1, x4)
    dm = each(lambda a, b: a + b + _bdot(a, b), a2, x8)
    y = each(lambda d, l_: -(l_ + _bdot(d, l_)), dm, lo)
    y2 = each(_bdot, y, y)
    qm = each(lambda a, b: a + b + _bdot(a, b), y, y2)
    tm = each(lambda a, b: a + b + _bdot(a, b), qm, dm)
    eg = each(jnp.exp, gc)
    uw = each(lambda v_, b_, kb_, e_: jnp.concatenate([v_ * b_, kb_ * e_], axis=1), v, beta, kb, eg)
    uw = each(lambda t_, u_: (u_ + _bdot(t_, u_)).astype(BF16), tm, uw)
    au = each(_dot, a_qk, uw)
    kt = each(lambda k_, g_: (k_ * jnp.exp(g_[c - 1:c, :] - g_)).astype(BF16), k, gc)
    kw = each(_dot_tn, kt, uw)
    for u in range(len(units)):
        op_ref[u] = au[u][:, :hd]
        qp_ref[u] = (q[u] * eg[u] - au[u][:, hd:]).astype(BF16)
        r_ref[u] = kw[u][:, :hd]
        w2_ref[u] = kw[u][:, hd:].astype(BF16)
        gl_ref[u] = jnp.exp(gc[u][c - 1:c, :])

    s = [state_ref[hh] for hh in range(hps)]
    for u, (n, hh) in enumerate(units):
        sb = s[hh].astype(BF16)
        os_ref[n * c:(n + 1) * c, lanes(hh)] = op_ref[u] + _dot(qp_ref[u], sb)
        s[hh] = s[hh] * gl_ref[u] + r_ref[u] - _dot(w2_ref[u], sb)
    for hh in range(hps):
        state_ref[hh] = s[hh]
        z = z_ref[:, lanes(hh)]
        o_ref[:, lanes(hh)] = (_rms(os_ref[:, lanes(hh)], nw_ref[...]) * (z * jax.nn.sigmoid(z))).astype(o_ref.dtype)


def _gdn(proj_g, proj_ab, conv_w, a_log, dt_bias, norm_w, batch, seq, *, tt, hps):
    m = proj_g.shape[0]
    nt = seq // tt
    hd = HEAD_DIM
    wide = hps * hd
    ng = GDN_HEADS // hps
    n_units = (tt // GDN_CHUNK) * hps
    row = lambda b, hg, t: b * nt + t
    on_a_lanes = lambda p: jnp.pad(p.astype(F32), (0, hd - GDN_HEADS)).reshape(1, hd)
    tile = lambda part: pl.BlockSpec((tt, wide), lambda b, hg, t: (row(b, hg, t), part * ng + hg))
    cw = lambda part: pl.BlockSpec((CONV_WIDTH, wide), lambda b, hg, t: (0, part * ng + hg))
    per_head = pl.BlockSpec((1, hd), lambda b, hg, t: (0, 0))
    return pl.pallas_call(
        _gdn_kernel,
        grid=(batch, ng, nt),
        in_specs=[
            tile(0), tile(1), tile(2), tile(3),
            pl.BlockSpec((tt, hd), lambda b, hg, t: (row(b, hg, t), 0)),
            cw(0), cw(1), cw(2),
            per_head, per_head,
            pl.BlockSpec((1, hd), lambda b, hg, t: (0, 0)),
        ],
        out_specs=pl.BlockSpec((tt, wide), lambda b, hg, t: (row(b, hg, t), hg)),
        out_shape=jax.ShapeDtypeStruct((m, GDN_HEADS * hd), BF16),
        scratch_shapes=[
            pltpu.VMEM((hps, hd, hd), F32),
            pltpu.VMEM((3, 8, wide), F32),
            pltpu.VMEM((tt + 8, wide), F32),
            pltpu.VMEM((tt, wide), F32),
            pltpu.VMEM((tt, wide), F32),
            pltpu.VMEM((tt, wide), F32),
            pltpu.VMEM((tt, wide), F32),
            pltpu.VMEM((tt, wide), F32),
            pltpu.VMEM((tt, wide), F32),
            pltpu.VMEM((n_units, hd, hd), BF16),
            pltpu.VMEM((n_units, hd, hd), F32),
            pltpu.VMEM((n_units, GDN_CHUNK, hd), BF16),
            pltpu.VMEM((n_units, GDN_CHUNK, hd), F32),
            pltpu.VMEM((n_units, 1, hd), F32),
        ],
        compiler_params=_params(("parallel", "parallel", "arbitrary")),
        name="gdn",
    )(proj_g, proj_g, proj_g, proj_g, proj_ab, conv_w, conv_w, conv_w,
      on_a_lanes(a_log), on_a_lanes(dt_bias), norm_w.reshape(1, hd))


def _moba_kernel(q_ref, k_ref, v_ref, slope_ref, o_ref,
                 kaug_ref, qaug_ref, kmean_ref, kparts_ref, sel_ref, sa_ref, sb_ref, acc_ref, vt_ref, ta_ref, tb_ref):
    blk = MOBA_BLOCK
    hd = HEAD_DIM
    grp = MOBA_GROUP
    hps = q_ref.shape[1] // hd
    heads = range(hps)
    nb = k_ref.shape[0] // blk
    i = pl.program_id(2)
    lanes = lambda hh: slice(hh * hd, (hh + 1) * hd)
    slope2 = [slope_ref[hh] for hh in heads]

    @pl.when(i == 0)
    def _():
        pos = lax.broadcasted_iota(jnp.int32, (blk, hd), 0).astype(F32)
        lane = lax.broadcasted_iota(jnp.int32, (blk, hd), 1)
        one = jnp.ones((blk, hd), BF16)
        zero = jnp.zeros((blk, hd), BF16)
        ones_row = (lax.broadcasted_iota(jnp.int32, (MOBA_VROWS - hd, blk), 0) == 0).astype(BF16)
        k_extra = []
        for hh in heads:
            hi, mid, lo = _split3(slope2[hh][:, :hd] * pos)
            pieces = jnp.where(lane == 0, hi, jnp.where(lane == 1, mid, jnp.where(lane == 2, lo, zero)))
            k_extra.append(jnp.where((lane >= 3) & (lane < 6), one, pieces))
            npieces = jnp.where(lane == 3, -hi, jnp.where(lane == 4, -mid, jnp.where(lane == 5, -lo, zero)))
            qaug_ref[hh, :, hd:] = jnp.where(lane < 3, one, npieces)

        def block_step(j, carry):
            r0 = pl.multiple_of(j * blk, blk)
            for hh in heads:
                kb = k_ref[pl.ds(r0, blk), lanes(hh)]
                kaug_ref[hh, pl.ds(r0, blk), 0:hd] = kb
                kaug_ref[hh, pl.ds(r0, blk), hd:] = k_extra[hh]
                kmean_ref[hh, pl.ds(j, 1), :] = jnp.mean(kb.astype(F32), axis=0, keepdims=True)
                vt_ref[hh, j, 0:hd, :] = v_ref[pl.ds(r0, blk), lanes(hh)].astype(F32).T.astype(BF16)
                vt_ref[hh, j, hd:, :] = ones_row
            return carry
        lax.fori_loop(0, nb, block_step, 0, unroll=2)
        for hh in heads:
            for part, piece in enumerate(_split3(kmean_ref[hh])):
                kparts_ref[hh, part * nb:(part + 1) * nb, :] = piece

    blk_id = lax.broadcasted_iota(jnp.int32, (nb, blk), 0)
    key_pos = lax.broadcasted_iota(jnp.int32, (blk, blk), 0)
    qry_pos = lax.broadcasted_iota(jnp.int32, (blk, blk), 1)
    r0 = pl.multiple_of(i * blk, blk)

    def score_group(g, dst_ref, top_ref):
        for b in range(grp):
            c0 = pl.multiple_of((g * grp + b) * blk, blk)
            for hh in heads:
                s2 = _dot_nt(kaug_ref[hh, pl.ds(c0, blk), :], qa[hh])
                dst_ref[hh * grp + b] = s2
                top_ref[hh * grp + b] = jnp.max(s2, axis=0, keepdims=True)

    qa, gates, own = [], [], []
    for hh in heads:
        q = q_ref[:, lanes(hh)]
        qaug_ref[hh, :, 0:hd] = q
        qa.append(qaug_ref[hh])
        g3 = _dot_nt(kparts_ref[hh], q)
        gates.append(g3[0:nb] + g3[nb:2 * nb] + g3[2 * nb:3 * nb])
        own.append(_dot_nt(kaug_ref[hh, pl.ds(r0, blk), :], qa[hh]))
    score_group(0, sa_ref, ta_ref)
    score_group(1, sb_ref, tb_ref)

    start = []
    for hh in heads:
        gate = jnp.where(blk_id < i, gates[hh], NEG_INF)
        sel = jnp.zeros((nb, blk), F32)
        for _ in range(MOBA_TOPK):
            best = jnp.max(gate, axis=0, keepdims=True)
            hit = (gate == best) & (best > NEG_INF)
            first = jnp.min(jnp.where(hit, blk_id, nb), axis=0, keepdims=True)
            pick = blk_id == first
            sel = jnp.where(pick, 1.0, sel)
            gate = jnp.where(pick, NEG_INF, gate)
        sel_ref[hh] = sel
        s = jnp.where(key_pos <= qry_pos, own[hh], NEG_INF)
        m0 = jnp.max(s, axis=0, keepdims=True)
        p = jnp.exp2(s - m0)
        acc_ref[hh] = _dot(vt_ref[hh, i], p.astype(BF16))
        start.append(m0)

    def group_max(g, top_ref, carry):
        subs = [[] for _ in heads]
        m_new = [carry[hh] for hh in heads]
        for b in range(grp):
            j = g * grp + b
            for hh in heads:
                chosen = sel_ref[hh, pl.ds(j, 1), :] > 0.0
                off = slope2[hh] * ((i - j) * blk).astype(F32)
                top = top_ref[hh * grp + b] - off
                m_new[hh] = jnp.maximum(m_new[hh], jnp.where(chosen, top, NEG_INF))
                subs[hh].append((chosen, off))
        return m_new, subs

    last_group = nb // grp - 1
    n_groups = (i + grp - 1) // grp
    n_pairs = n_groups // 2

    def attend_group(g, src_ref, top_ref, carry, refill):
        m_new, subs = group_max(g, top_ref, carry)
        alpha = [jnp.exp2(carry[hh] - m_new[hh]) for hh in heads]
        nxt = jnp.minimum(g + 2, last_group)
        pv = [None for _ in heads]
        for b in range(grp):
            c0 = pl.multiple_of((nxt * grp + b) * blk, blk)
            for hh in heads:
                chosen, off = subs[hh][b]
                p2 = jnp.exp2((src_ref[hh * grp + b] - jnp.where(chosen, m_new[hh] + off, jnp.inf)).astype(BF16))
                part = _dot(vt_ref[hh, g * grp + b], p2)
                pv[hh] = part if pv[hh] is None else pv[hh] + part
                if refill:
                    s2 = _dot_nt(kaug_ref[hh, pl.ds(c0, blk), :], qa[hh])
                    src_ref[hh * grp + b] = s2
                    top_ref[hh * grp + b] = jnp.max(s2, axis=0, keepdims=True)
        for hh in heads:
            acc_ref[hh] = acc_ref[hh] * alpha[hh] + pv[hh]
        return tuple(m_new)

    def past_pair(gp, carry, refill_odd=True):
        carry = attend_group(2 * gp, sa_ref, ta_ref, carry, True)
        return attend_group(2 * gp + 1, sb_ref, tb_ref, carry, refill_odd)

    final = lax.fori_loop(0, jnp.maximum(n_pairs - 1, 0), past_pair, tuple(start))
    final = lax.cond(n_pairs > 0, lambda c: past_pair(n_pairs - 1, c, False), lambda c: c, final)

    @pl.when(n_groups % 2 == 1)
    def _():
        attend_group(n_groups - 1, sa_ref, ta_ref, final, False)

    for hh in heads:
        o_ref[:, lanes(hh)] = (acc_ref[hh, 0:hd, :] / acc_ref[hh, hd:hd + 1, :]).T.astype(o_ref.dtype)


def _moba(proj_m, slopes, batch, seq, *, hps):
    m = proj_m.shape[0]
    blk, hd = MOBA_BLOCK, HEAD_DIM
    wide = hps * hd
    ng = MOBA_HEADS // hps
    nb = seq // blk
    assert nb % (2 * MOBA_GROUP) == 0
    slope_b = jnp.broadcast_to((slopes.astype(F32) * LOG2E)[:, None, None], (MOBA_HEADS, 1, blk))
    whole_seq = lambda part: pl.BlockSpec((seq, wide), lambda b, hg, i: (b, part * ng + hg),
                                          pipeline_mode=pl.Buffered(1))
    return pl.pallas_call(
        _moba_kernel,
        grid=(batch, ng, nb),
        in_specs=[
            pl.BlockSpec((blk, wide), lambda b, hg, i: (b * nb + i, hg)),
            whole_seq(1),
            whole_seq(2),
            pl.BlockSpec((hps, 1, blk), lambda b, hg, i: (hg, 0, 0)),
        ],
        out_specs=pl.BlockSpec((blk, wide), lambda b, hg, i: (b * nb + i, hg)),
        out_shape=jax.ShapeDtypeStruct((m, MOBA_HEADS * hd), BF16),
        scratch_shapes=[
            pltpu.VMEM((hps, seq, 2 * hd), BF16),
            pltpu.VMEM((hps, blk, 2 * hd), BF16),
            pltpu.VMEM((hps, nb, hd), F32),
            pltpu.VMEM((hps, 3 * nb, hd), BF16),
            pltpu.VMEM((hps, nb, blk), F32),
            pltpu.VMEM((hps * MOBA_GROUP, blk, blk), F32),
            pltpu.VMEM((hps * MOBA_GROUP, blk, blk), F32),
            pltpu.VMEM((hps, MOBA_VROWS, blk), F32),
            pltpu.VMEM((hps, nb, MOBA_VROWS, blk), BF16),
            pltpu.VMEM((hps * MOBA_GROUP, 1, blk), F32),
            pltpu.VMEM((hps * MOBA_GROUP, 1, blk), F32),
        ],
        compiler_params=_params(("parallel", "parallel", "arbitrary")),
        name="moba",
    )(proj_m, proj_m, proj_m, slope_b)


def _mem_attn_kernel(x_ref, gpre_ref, wq_ref, kv_ref, wo_ref, gpost_ref, o_ref):
    hd = HEAD_DIM
    width = MEM_HEADS * hd
    kv = kv_ref[...]
    step = x_ref.shape[0] // 2
    for r in range(2):
        rows = slice(r * step, (r + 1) * step)
        x = x_ref[rows, :]
        hn = _rms(x, gpre_ref[...]).astype(BF16)
        q = (_dot(hn, wq_ref[...]) * (hd ** -0.5)).astype(BF16)
        heads = []
        for hh in range(MEM_HEADS):
            s = _dot_nt(q[:, hh * hd:(hh + 1) * hd], kv[:, hh * hd:(hh + 1) * hd])
            p = jnp.exp(s - jnp.max(s, axis=-1, keepdims=True))
            p = p / jnp.sum(p, axis=-1, keepdims=True)
            heads.append(_dot(p.astype(BF16), kv[:, width + hh * hd:width + (hh + 1) * hd]))
        o = jnp.concatenate(heads, axis=-1).astype(BF16)
        o_ref[rows, :] = x + _rms(_dot(o, wo_ref[...]), gpost_ref[...])


def _mem_attn(x2d, g_pre, wq_bf16, kv_bf16, wo_bf16, g_post, batch, seq, *, tm):
    m, d = x2d.shape
    nt = seq // tm
    mem_len, kvw = kv_bf16.shape[1], kv_bf16.shape[2]
    width = wq_bf16.shape[1]
    return pl.pallas_call(
        _mem_attn_kernel,
        grid=(batch, nt),
        in_specs=[
            pl.BlockSpec((tm, d), lambda b, t: (b * nt + t, 0)),
            pl.BlockSpec((1, d), lambda b, t: (0, 0)),
            pl.BlockSpec((d, width), lambda b, t: (0, 0)),
            pl.BlockSpec((None, mem_len, kvw), lambda b, t: (b, 0, 0)),
            pl.BlockSpec((width, d), lambda b, t: (0, 0)),
            pl.BlockSpec((1, d), lambda b, t: (0, 0)),
        ],
        out_specs=pl.BlockSpec((tm, d), lambda b, t: (b * nt + t, 0)),
        out_shape=jax.ShapeDtypeStruct((m, d), F32),
        compiler_params=_params(("parallel", "parallel")),
        name="mem_attn",
    )(x2d, g_pre.reshape(1, d), wq_bf16, kv_bf16, wo_bf16, g_post.reshape(1, d))


def kernel(x, mem, pre_mix_norm, w_in, conv_w, a_log, dt_bias, gdn_norm_w, w_out, post_mix_norm,
           pre_mem_norm, mem_kv_norm, w_mq, w_mk, w_mv, w_mo, post_mem_norm,
           pre_mlp_norm, w_up, w_down, post_mlp_norm):
    batch, seq, d = x.shape
    mem_len = mem.shape[1]
    gw = GDN_HEADS * HEAD_DIM
    mw = MOBA_HEADS * HEAD_DIM
    assert seq % MOBA_BLOCK == 0 and seq % 512 == 0
    x2d = x.reshape(batch * seq, d)
    mem2d = mem.reshape(batch * mem_len, d)
    slopes = jnp.exp2(-8.0 * jnp.arange(1, MOBA_HEADS + 1, dtype=F32) / MOBA_HEADS)
    ones = lambda n: jnp.ones((n,), F32)

    for l in range(w_in.shape[0]):
        win = w_in[l]
        w_g = win[:, :4 * gw].astype(BF16)
        w_ab = jnp.pad(win[:, 4 * gw:4 * gw + 2 * GDN_HEADS], ((0, 0), (0, HEAD_DIM - 2 * GDN_HEADS))).astype(BF16)
        w_m = win[:, 4 * gw + 2 * GDN_HEADS:].astype(BF16)
        q_scale = jnp.concatenate([jnp.full((mw,), LOG2E * HEAD_DIM ** -0.5, F32), ones(2 * mw)])

        proj_g, proj_ab = _norm_matmul(x2d, pre_mix_norm[l], w_g, ones(4 * gw), F32, tm=1024, tn=1024,
                                       side_w=w_ab, name="in_proj_gdn")
        proj_m = _norm_matmul(x2d, pre_mix_norm[l], w_m, q_scale, BF16, tm=1024, tn=1536, name="in_proj_moba")

        y_gdn = _gdn(proj_g, proj_ab, conv_w[l], a_log[l], dt_bias[l], gdn_norm_w[l], batch, seq, tt=256, hps=8)
        y_moba = _moba(proj_m, slopes, batch, seq, hps=2)
        x2d = _matmul_norm_res([y_gdn, y_moba], w_out[l].astype(BF16), post_mix_norm[l], x2d,
                               tm=512, tk=gw + mw, name="out_proj")

        w_kv = jnp.concatenate([w_mk[l], w_mv[l]], axis=1).astype(BF16)
        kv = _norm_matmul(mem2d, mem_kv_norm[l], w_kv, ones(w_kv.shape[1]), BF16, tm=512, tn=1024, name="mem_kv")
        x2d = _mem_attn(x2d, pre_mem_norm[l], w_mq[l].astype(BF16), kv.reshape(batch, mem_len, -1),
                        w_mo[l].astype(BF16), post_mem_norm[l], batch, seq, tm=1024)

        hid = _norm_matmul(x2d, pre_mlp_norm[l], w_up[l].astype(BF16), ones(w_up.shape[2]), BF16,
                           tm=1024, tn=2048, relu2=True, name="mlp_up")
        x2d = _matmul_norm_res([hid], w_down[l].astype(BF16), post_mlp_norm[l], x2d, tm=512, tk=2048, name="mlp_down")
    return x2d.reshape(batch, seq, d)
```
